```python
import math
import jax, jax.numpy as jnp
from jax import lax
import numpy as np

D_MODEL = 1024
BATCH = 4
SEQ = 4096
DEPTH = 1

GDN_HEADS = 4
GDN_HEAD_DIM = 128
MLSTM_HEADS = 4
MLSTM_HEAD_DIM = 128
GDN_W = GDN_HEADS * GDN_HEAD_DIM
MLSTM_W = MLSTM_HEADS * MLSTM_HEAD_DIM
CONV_WIDTH = 4
GDN_CHUNK = 64
MLSTM_CHUNK = 64
PROJ_SIZES = (3 * GDN_W, GDN_W, GDN_HEADS, GDN_HEADS,
              2 * MLSTM_W, MLSTM_W, MLSTM_W, MLSTM_HEADS, MLSTM_HEADS)
PROJ_DIM = sum(PROJ_SIZES)
MEM_TOKENS = 256
XA_HEADS = 4
XA_HEAD_DIM = D_MODEL // XA_HEADS
PEER_HEADS = 8
PEER_N_KEYS = 128
PEER_N_EXPERTS = PEER_N_KEYS * PEER_N_KEYS
PEER_TOPK = 16
PEER_HALF = 128
PEER_QUERY_DIM = 2 * PEER_HALF
PEER_BLOCK = 128
NORM_EPS = 1e-6

kernel_name = 'hymba_gdn_mlstm_peer_block'


def rmsnorm(x, w):
    xf = x.astype(jnp.float32)
    y = xf * lax.rsqrt(jnp.mean(xf * xf, axis=-1, keepdims=True) + NORM_EPS)
    return (y * w.astype(jnp.float32)).astype(x.dtype)


def l2norm(x):
    return x * lax.rsqrt(jnp.sum(x * x, axis=-1, keepdims=True) + NORM_EPS)


def causal_dwconv(x, w):
    kw = w.shape[0]
    return lax.conv_general_dilated(x, w[:, None, :].astype(x.dtype), window_strides=(1,),
                                    padding=[(kw - 1, 0)],
                                    dimension_numbers=('NWC', 'WIO', 'NWC'),
                                    feature_group_count=x.shape[-1])


def to_chunks(t, nc, lc):
    b, h = t.shape[0], t.shape[1]
    return jnp.moveaxis(t.reshape(b, h, nc, lc, *t.shape[3:]), 2, 0)


def gated_delta_rule(q, k, v, log_alpha, beta):
    b, nh, s, dk = q.shape
    dv = v.shape[-1]
    lc = GDN_CHUNK
    nc = s // lc
    q = q * dk ** -0.5
    q, k, v, log_alpha, beta = (to_chunks(t, nc, lc) for t in (q, k, v, log_alpha, beta))
    gc = jnp.cumsum(log_alpha, axis=-1)
    idx = jnp.arange(lc)
    causal = idx[:, None] >= idx[None, :]
    strict = idx[:, None] > idx[None, :]
    decay = jnp.exp(jnp.where(causal, gc[..., :, None] - gc[..., None, :], -jnp.inf))
    kb = k * beta[..., None]
    a = jnp.where(strict, jnp.einsum('nbhid,nbhjd->nbhij', kb, k) * decay, 0.0)
    eye = jnp.eye(lc, dtype=q.dtype)
    t_inv = lax.linalg.triangular_solve(a + eye, jnp.broadcast_to(eye, a.shape), left_side=True,
                                        lower=True, unit_diagonal=True)
    u = t_inv @ (v * beta[..., None])
    w = t_inv @ (kb * jnp.exp(gc)[..., None])
    qk = jnp.einsum('nbhid,nbhjd->nbhij', q, k) * decay
    q_dec = q * jnp.exp(gc)[..., None]
    k_dec = k * jnp.exp(gc[..., -1:] - gc)[..., None]
    g_last = jnp.exp(gc[..., -1])

    def step(state, xs):
        u_c, w_c, qk_c, qd_c, kd_c, gl_c = xs
        v_new = u_c - w_c @ state
        o = qd_c @ state + qk_c @ v_new
        state = state * gl_c[..., None, None] + jnp.einsum('bhld,bhle->bhde', kd_c, v_new)
        return state, o

    s0 = jnp.zeros((b, nh, dk, dv), q.dtype)
    _, o = lax.scan(step, s0, (u, w, qk, q_dec, k_dec, g_last))
    return jnp.moveaxis(o, 0, 2).reshape(b, nh, s, dv)


def mlstm_chunkwise(q, k, v, log_i, log_f):
    b, nh, s, dk = q.shape
    dv = v.shape[-1]
    lc = MLSTM_CHUNK
    nc = s // lc
    k = k * dk ** -0.5
    q, k, v, log_i, log_f = (to_chunks(t, nc, lc) for t in (q, k, v, log_i, log_f))
    bcum = jnp.cumsum(log_f, axis=-1)
    idx = jnp.arange(lc)
    causal = idx[:, None] >= idx[None, :]
    d_intra = jnp.where(causal, bcum[..., :, None] - bcum[..., None, :] + log_i[..., None, :], -jnp.inf)
    d_last = bcum[..., -1:] - bcum + log_i
    qk = jnp.einsum('nbhid,nbhjd->nbhij', q, k)

    def step(carry, xs):
        c, n, m = carry
        q_c, k_c, v_c, b_c, d_c, qk_c, dl_c = xs
        inter = b_c + m[..., None]
        m_t = jnp.maximum(inter, jnp.max(d_c, axis=-1))
        p = qk_c * jnp.exp(d_c - m_t[..., None])
        sc = jnp.exp(inter - m_t)
        num = sc[..., None] * (q_c @ c) + p @ v_c
        den = sc * jnp.einsum('bhld,bhd->bhl', q_c, n) + jnp.sum(p, axis=-1)
        h = num / jnp.maximum(jnp.abs(den), jnp.exp(-m_t))[..., None]
        b_last = b_c[..., -1]
        m_new = jnp.maximum(b_last + m, jnp.max(dl_c, axis=-1))
        wgt = jnp.exp(dl_c - m_new[..., None])
        dec = jnp.exp(b_last + m - m_new)
        c = dec[..., None, None] * c + jnp.einsum('bhld,bhle->bhde', k_c * wgt[..., None], v_c)
        n = dec[..., None] * n + jnp.einsum('bhld,bhl->bhd', k_c, wgt)
        return (c, n, m_new), h

    init = (jnp.zeros((b, nh, dk, dv), q.dtype), jnp.zeros((b, nh, dk), q.dtype),
            jnp.zeros((b, nh), q.dtype))
    _, hs = lax.scan(step, init, (q, k, v, bcum, d_intra, qk, d_last))
    return jnp.moveaxis(hs, 0, 2).reshape(b, nh, s, dv)


def hybrid_mixer(h, w_in, gdn_conv_w, gdn_a_log, gdn_dt_bias, gdn_norm_w,
                 mlstm_conv_w, mlstm_i_bias, mlstm_f_bias, mlstm_norm_w, w_out):
    b, s, _ = h.shape
    f32 = jnp.float32
    proj = (h @ w_in).astype(f32)
    cuts = np.cumsum(PROJ_SIZES)[:-1].tolist()
    g_qkv, g_z, g_a, g_b, m_qk, m_v, m_o, m_i, m_f = jnp.split(proj, cuts, axis=-1)

    def heads(t, nh):
        return t.reshape(b, s, nh, -1).transpose(0, 2, 1, 3)

    g_qkv = jax.nn.silu(causal_dwconv(g_qkv, gdn_conv_w.astype(f32)))
    gq, gk, gv = (heads(t, GDN_HEADS) for t in jnp.split(g_qkv, 3, axis=-1))
    gq, gk = l2norm(gq), l2norm(gk)
    log_alpha = -jnp.exp(gdn_a_log.astype(f32)) * jax.nn.softplus(g_a + gdn_dt_bias.astype(f32))
    beta = jax.nn.sigmoid(g_b)
    go = gated_delta_rule(gq, gk, gv, log_alpha.transpose(0, 2, 1), beta.transpose(0, 2, 1))
    go = rmsnorm(go.transpose(0, 2, 1, 3), gdn_norm_w) * jax.nn.silu(
        g_z.reshape(b, s, GDN_HEADS, GDN_HEAD_DIM))

    m_qk = jax.nn.silu(causal_dwconv(m_qk, mlstm_conv_w.astype(f32)))
    mq, mk = (heads(t, MLSTM_HEADS) for t in jnp.split(m_qk, 2, axis=-1))
    mv = heads(m_v, MLSTM_HEADS)
    log_i = (m_i + mlstm_i_bias.astype(f32)).transpose(0, 2, 1)
    log_f = jax.nn.log_sigmoid(m_f + mlstm_f_bias.astype(f32)).transpose(0, 2, 1)
    mh = mlstm_chunkwise(mq, mk, mv, log_i, log_f)
    mh = rmsnorm(mh.transpose(0, 2, 1, 3), mlstm_norm_w.reshape(MLSTM_HEADS, MLSTM_HEAD_DIM)) * \
        jax.nn.sigmoid(m_o.reshape(b, s, MLSTM_HEADS, MLSTM_HEAD_DIM))

    mixed = jnp.concatenate([go.reshape(b, s, GDN_W), mh.reshape(b, s, MLSTM_W)], axis=-1)
    return mixed.astype(h.dtype) @ w_out


def memory_cross_attention(h, m, wq, wkv, wo):
    b, s, _ = h.shape
    q = (h @ wq).reshape(b, s, XA_HEADS, XA_HEAD_DIM)
    k, v = jnp.split(m @ wkv, 2, axis=-1)
    k = k.reshape(b, -1, XA_HEADS, XA_HEAD_DIM)
    v = v.reshape(b, -1, XA_HEADS, XA_HEAD_DIM)
    scores = jnp.einsum('bshd,bmhd->bhsm', q, k).astype(jnp.float32) * XA_HEAD_DIM ** -0.5
    p = jax.nn.softmax(scores, axis=-1).astype(v.dtype)
    o = jnp.einsum('bhsm,bmhd->bshd', p, v).reshape(b, s, D_MODEL)
    return o @ wo


def peer_ffn(h, wq, sub_keys, u_table, v_table):
    b, s, d = h.shape
    t = b * s
    hf = h.reshape(t, d)
    q = (hf @ wq).reshape(t, PEER_HEADS, 2, PEER_HALF)
    sc = jnp.einsum('thpd,hpnd->thpn', q, sub_keys).astype(jnp.float32)
    top_v, top_i = lax.top_k(sc, PEER_TOPK)
    kk = PEER_TOPK * PEER_TOPK
    cand = (top_v[:, :, 0, :, None] + top_v[:, :, 1, None, :]).reshape(t, PEER_HEADS, kk)
    cand_id = (top_i[:, :, 0, :, None] * PEER_N_KEYS + top_i[:, :, 1, None, :]).reshape(t, PEER_HEADS, kk)
    best_v, best_pos = lax.top_k(cand, PEER_TOPK)
    expert_id = jnp.take_along_axis(cand_id, best_pos, axis=-1)
    gate = jax.nn.softmax(best_v, axis=-1)
    nb = t // PEER_BLOCK

    def block(args):
        xb, eb, gb = args
        u = jnp.take(u_table, eb, axis=0)
        act = jax.nn.gelu(jnp.einsum('phkd,pd->phk', u, xb).astype(jnp.float32), approximate=False)
        wgt = (gb * act).astype(v_table.dtype)
        return jnp.einsum('phk,phkd->pd', wgt, jnp.take(v_table, eb, axis=0))

    out = lax.map(block, (hf.reshape(nb, PEER_BLOCK, d),
                          expert_id.reshape(nb, PEER_BLOCK, PEER_HEADS, PEER_TOPK),
                          gate.reshape(nb, PEER_BLOCK, PEER_HEADS, PEER_TOPK)))
    return out.reshape(b, s, d).astype(h.dtype)


def setup_inputs(seed: int = 0) -> dict:
    key = jax.random.key(seed)
    ks = jax.random.split(key, 24)
    f32 = jnp.float32
    L, D = DEPTH, D_MODEL

    def nrm(k, shape, scale):
        return jax.random.normal(k, shape, f32) * scale

    def gain(k, shape):
        return 1.0 + 0.01 * jax.random.normal(k, shape, f32)

    dt = jnp.exp(jax.random.uniform(ks[6], (L, GDN_HEADS), f32, math.log(1e-3), math.log(1e-1)))
    return {
        'x': nrm(ks[0], (BATCH, SEQ, D), 1.0),
        'mem': nrm(ks[1], (BATCH, MEM_TOKENS, D), 1.0),
        'norm_mix_w': gain(ks[2], (L, D)),
        'w_in': nrm(ks[3], (L, D, PROJ_DIM), D ** -0.5),
        'gdn_conv_w': nrm(ks[4], (L, CONV_WIDTH, 3 * GDN_W), CONV_WIDTH ** -0.5),
        'gdn_a_log': jnp.log(jax.random.uniform(ks[5], (L, GDN_HEADS), f32, 1.0, 16.0)),
        'gdn_dt_bias': dt + jnp.log(-jnp.expm1(-dt)),
        'gdn_norm_w': gain(ks[7], (L, GDN_HEAD_DIM)),
        'mlstm_conv_w': nrm(ks[8], (L, CONV_WIDTH, 2 * MLSTM_W), CONV_WIDTH ** -0.5),
        'mlstm_i_bias': nrm(ks[9], (L, MLSTM_HEADS), 0.5),
        'mlstm_f_bias': jax.random.uniform(ks[10], (L, MLSTM_HEADS), f32, 3.0, 6.0),
        'mlstm_norm_w': gain(ks[11], (L, MLSTM_W)),
        'w_out': nrm(ks[12], (L, GDN_W + MLSTM_W, D), (GDN_W + MLSTM_W) ** -0.5),
        'norm_xa_w': gain(ks[13], (L, D)),
        'norm_mem_w': gain(ks[14], (L, D)),
        'xa_wq': nrm(ks[15], (L, D, D), D ** -0.5),
        'xa_wkv': nrm(ks[16], (L, D, 2 * D), D ** -0.5),
        'xa_wo': nrm(ks[17], (L, D, D), D ** -0.5),
        'norm_ffn_w': gain(ks[18], (L, D)),
        'peer_wq': nrm(ks[19], (L, D, PEER_HEADS * PEER_QUERY_DIM), D ** -0.5),
        'peer_sub_keys': nrm(ks[20], (L, PEER_HEADS, 2, PEER_N_KEYS, PEER_HALF), PEER_HALF ** -0.5),
        'peer_u': nrm(ks[21], (L, PEER_N_EXPERTS, D), D ** -0.5),
        'peer_v': nrm(ks[22], (L, PEER_N_EXPERTS, D), (PEER_HEADS * PEER_TOPK) ** -0.5),
        'norm_final_w': gain(ks[23], (D,)),
    }


def reference(x, mem, norm_mix_w, w_in, gdn_conv_w, gdn_a_log, gdn_dt_bias, gdn_norm_w,
              mlstm_conv_w, mlstm_i_bias, mlstm_f_bias, mlstm_norm_w, w_out,
              norm_xa_w, norm_mem_w, xa_wq, xa_wkv, xa_wo,
              norm_ffn_w, peer_wq, peer_sub_keys, peer_u, peer_v, norm_final_w):
    for l in range(DEPTH):
        x = x + hybrid_mixer(rmsnorm(x, norm_mix_w[l]), w_in[l], gdn_conv_w[l], gdn_a_log[l],
                             gdn_dt_bias[l], gdn_norm_w[l], mlstm_conv_w[l], mlstm_i_bias[l],
                             mlstm_f_bias[l], mlstm_norm_w[l], w_out[l])
        x = x + memory_cross_attention(rmsnorm(x, norm_xa_w[l]), rmsnorm(mem, norm_mem_w[l]),
                                       xa_wq[l], xa_wkv[l], xa_wo[l])
        x = x + peer_ffn(rmsnorm(x, norm_ffn_w[l]), peer_wq[l], peer_sub_keys[l], peer_u[l], peer_v[l])
    return rmsnorm(x, norm_final_w)
```

```python
import functools
import math

import jax
import jax.numpy as jnp
from jax import lax
from jax.experimental import pallas as pl
from jax.experimental.pallas import tpu as pltpu

F32 = jnp.float32
BF16 = jnp.bfloat16
HIGHEST = lax.Precision.HIGHEST

D_MODEL = 1024
HEAD_DIM = 128
MIX_HEADS = 4
MIX_W = MIX_HEADS * HEAD_DIM
CONV_WIDTH = 4
CHUNK = 64
MEM_TOKENS = 256
XA_HEADS = 4
XA_HEAD_DIM = D_MODEL // XA_HEADS
PEER_HEADS = 8
PEER_KEYS = 128
PEER_TOPK = 16
PEER_HALF = 128
NORM_EPS = 1e-6
NEG_BIG = -1e30

CONV_COLS = 3 * MIX_W + 2 * MIX_W
PLAIN_COLS = 3 * MIX_W
GATE_COLS = 128
PROJ_COLS = CONV_COLS + PLAIN_COLS + GATE_COLS
HALO = 8

VMEM_LIMIT = 56 * 1024 * 1024

NT_DIMS = (((1,), (1,)), ((), ()))
TN_DIMS = (((0,), (0,)), ((), ()))


def _rms(x, w):
    return x * lax.rsqrt(jnp.mean(x * x, axis=-1, keepdims=True) + NORM_EPS) * w


def _sigmoid(x):
    return 1.0 / (1.0 + jnp.exp(-x))


def _softplus(x):
    return jnp.maximum(x, 0.0) + jnp.log1p(jnp.exp(-jnp.abs(x)))


def _dot(a, b):
    return jnp.dot(a, b, preferred_element_type=F32)


def _dot_nt(a, b, precision=None):
    return lax.dot_general(a, b, NT_DIMS, precision=precision, preferred_element_type=F32)


def _dot_tn(a, b):
    return lax.dot_general(a, b, TN_DIMS, preferred_element_type=F32)


def _memkv_kernel(m_ref, nw_ref, w_ref, k_ref, v_ref):
    mn = _rms(m_ref[...], nw_ref[...]).astype(BF16)
    kv = _dot(mn, w_ref[...])
    k_ref[...] = kv[:, :D_MODEL].astype(BF16)
    v_ref[...] = kv[:, D_MODEL:].astype(BF16)


def _memkv(mem2d, nw, wkv):
    n = mem2d.shape[0]
    tm = MEM_TOKENS
    return pl.pallas_call(
        _memkv_kernel,
        out_shape=(jax.ShapeDtypeStruct((n, D_MODEL), BF16),
                   jax.ShapeDtypeStruct((n, D_MODEL), BF16)),
        grid=(n // tm,),
        in_specs=[pl.BlockSpec((tm, D_MODEL), lambda i: (i, 0)),
                  pl.BlockSpec((1, D_MODEL), lambda i: (0, 0)),
                  pl.BlockSpec((D_MODEL, 2 * D_MODEL), lambda i: (0, 0))],
        out_specs=(pl.BlockSpec((tm, D_MODEL), lambda i: (i, 0)),
                   pl.BlockSpec((tm, D_MODEL), lambda i: (i, 0))),
        compiler_params=pltpu.CompilerParams(dimension_semantics=("parallel",),
                                             vmem_limit_bytes=VMEM_LIMIT),
        name="memkv",
    )(mem2d, nw, wkv)


def _inproj_kernel(x_ref, halo_ref, nw_ref, w_ref, cw_ref, gbias_ref, alog_ref,
                   qkv_ref, zvo_ref, gate_ref, p_scr, *, tm, seq):
    i = pl.program_id(0)
    xc = jnp.concatenate([halo_ref[...], x_ref[...]], axis=0)
    xn = _rms(xc, nw_ref[...]).astype(BF16)
    p = _dot(xn, w_ref[...])

    first_row = jnp.where((i * tm) % seq == 0, HALO, 0)
    row = lax.broadcasted_iota(jnp.int32, (HALO + tm, 1), 0)
    p_scr[...] = jnp.where(row >= first_row, p[:, :CONV_COLS], 0.0)
    cw = cw_ref[...]
    y = cw[CONV_WIDTH - 1:CONV_WIDTH] * p_scr[HALO:HALO + tm, :]
    for kk in range(CONV_WIDTH - 1):
        off = HALO - (CONV_WIDTH - 1) + kk
        y = y + cw[kk:kk + 1] * p_scr[off:off + tm, :]
    y = y * _sigmoid(y)

    scale = HEAD_DIM ** -0.5
    for hh in range(2 * MIX_HEADS):
        cols = slice(hh * HEAD_DIM, (hh + 1) * HEAD_DIM)
        yh = y[:, cols]
        inv = lax.rsqrt(jnp.sum(yh * yh, axis=-1, keepdims=True) + NORM_EPS)
        if hh < MIX_HEADS:
            inv = inv * scale
        qkv_ref[:, cols] = yh * inv
    qkv_ref[:, 2 * MIX_W:4 * MIX_W] = y[:, 2 * MIX_W:4 * MIX_W]
    qkv_ref[:, 4 * MIX_W:5 * MIX_W] = y[:, 4 * MIX_W:5 * MIX_W] * scale

    zvo_ref[...] = p[HALO:, CONV_COLS:CONV_COLS + PLAIN_COLS]

    ga = p[HALO:, CONV_COLS + PLAIN_COLS:] + gbias_ref[...]
    lane = lax.broadcasted_iota(jnp.int32, (tm, GATE_COLS), 1)
    log_alpha = -jnp.exp(alog_ref[...]) * _softplus(ga)
    beta = _sigmoid(ga)
    log_f = -_softplus(-ga)
    gate_ref[...] = jnp.where(lane < 4, log_alpha,
                              jnp.where(lane < 8, beta,
                                        jnp.where(lane < 12, ga,
                                                  jnp.where(lane < 16, log_f, 0.0))))


def _inproj(x2d, nw, w_all, cw, gbias, alog, *, seq, tm):
    t = x2d.shape[0]
    hb = tm // HALO
    kern = functools.partial(_inproj_kernel, tm=tm, seq=seq)
    return pl.pallas_call(
        kern,
        out_shape=(jax.ShapeDtypeStruct((t, CONV_COLS), F32),
                   jax.ShapeDtypeStruct((t, PLAIN_COLS), F32),
                   jax.ShapeDtypeStruct((t, GATE_COLS), F32)),
        grid=(t // tm,),
        in_specs=[pl.BlockSpec((tm, D_MODEL), lambda i: (i, 0)),
                  pl.BlockSpec((HALO, D_MODEL), lambda i: (jnp.maximum(i * hb - 1, 0), 0)),
                  pl.BlockSpec((1, D_MODEL), lambda i: (0, 0)),
                  pl.BlockSpec((D_MODEL, PROJ_COLS), lambda i: (0, 0)),
                  pl.BlockSpec((CONV_WIDTH, CONV_COLS), lambda i: (0, 0)),
                  pl.BlockSpec((1, GATE_COLS), lambda i: (0, 0)),
                  pl.BlockSpec((1, GATE_COLS), lambda i: (0, 0))],
        out_specs=(pl.BlockSpec((tm, CONV_COLS), lambda i: (i, 0)),
                   pl.BlockSpec((tm, PLAIN_COLS), lambda i: (i, 0)),
                   pl.BlockSpec((tm, GATE_COLS), lambda i: (i, 0))),
        scratch_shapes=[pltpu.VMEM((HALO + tm, CONV_COLS), F32)],
        compiler_params=pltpu.CompilerParams(dimension_semantics=("parallel",),
                                             vmem_limit_bytes=VMEM_LIMIT),
        name="inproj",
    )(x2d, x2d, nw, w_all, cw, gbias, alog)


def _chunk_cumsum(g, ts):
    ri = lax.broadcasted_iota(jnp.int32, (ts, ts), 0)
    ci = lax.broadcasted_iota(jnp.int32, (ts, ts), 1)
    shift = CHUNK.bit_length() - 1
    same_chunk = jnp.right_shift(ri, shift) == jnp.right_shift(ci, shift)
    tri = jnp.where(jnp.logical_and(same_chunk, ci <= ri), 1.0, 0.0)
    return jnp.dot(tri, g, precision=HIGHEST, preferred_element_type=F32)


def _lane_pick(x, lane, idx):
    return jnp.where(lane == idx, x, 0.0)


def _gdn_kernel(q_ref, k_ref, v_ref, g_ref, o_ref, s_ref, *, ts):
    @pl.when(pl.program_id(1) == 0)
    def _():
        s_ref[...] = jnp.zeros_like(s_ref)

    L = CHUNK
    g = g_ref[...]
    cs = _chunk_cumsum(g, ts)
    lane = lax.broadcasted_iota(jnp.int32, (L, GATE_COLS), 1)
    r = lax.broadcasted_iota(jnp.int32, (L, L), 0)
    c = lax.broadcasted_iota(jnp.int32, (L, L), 1)
    causal = r >= c
    strict = r > c
    eye = (r == c).astype(F32)

    for ch in range(ts // L):
        rows = slice(ch * L, (ch + 1) * L)
        gch = g[rows]
        csch = cs[rows]
        for h in range(MIX_HEADS):
            cols = slice(h * HEAD_DIM, (h + 1) * HEAD_DIM)
            q = q_ref[rows, cols]
            k = k_ref[rows, cols]
            v = v_ref[rows, cols]
            e_h = (lane == h).astype(F32)
            cs_h = _lane_pick(csch, lane, h)
            diff = _dot_nt(jnp.concatenate([cs_h, e_h], axis=1),
                           jnp.concatenate([e_h, -cs_h], axis=1), precision=HIGHEST)
            decay = jnp.exp(jnp.where(causal, diff, NEG_BIG))
            gc = csch[:, h:h + 1]
            beta = gch[:, 4 + h:5 + h]
            gl = csch[L - 1:L, h:h + 1]
            eg = jnp.exp(gc)
            kb = k * beta
            k16 = k.astype(BF16)
            a = jnp.where(strict, _dot_nt(kb.astype(BF16), k16) * decay, 0.0)
            tinv = eye - a
            pw = a
            for _ in range(5):
                pw16 = pw.astype(BF16)
                pw = _dot(pw16, pw16)
                tinv = tinv + _dot(tinv.astype(BF16), pw.astype(BF16))
            rhs = jnp.concatenate([v * beta, kb * eg], axis=1).astype(BF16)
            uw = _dot(tinv.astype(BF16), rhs)
            u = uw[:, :HEAD_DIM]
            w = uw[:, HEAD_DIM:]
            q16 = q.astype(BF16)
            qk = _dot_nt(q16, k16) * decay
            qd = q * eg
            kd = k * jnp.exp(gl - gc)
            state = s_ref[h]
            rr = _dot(jnp.concatenate([w, qd], axis=0).astype(BF16), state.astype(BF16))
            vnew = u - rr[:L]
            vnew16 = vnew.astype(BF16)
            o_ref[rows, cols] = rr[L:] + _dot(qk.astype(BF16), vnew16)
            s_ref[h] = state * jnp.exp(gl) + _dot_tn(kd.astype(BF16), vnew16)


def _gdn(qkv, gates, *, batch, seq, ts):
    t = qkv.shape[0]
    nt = seq // ts
    kern = functools.partial(_gdn_kernel, ts=ts)

    def col(j):
        return pl.BlockSpec((ts, MIX_W), lambda b, s, j=j: (b * nt + s, j))

    return pl.pallas_call(
        kern,
        out_shape=jax.ShapeDtypeStruct((t, MIX_W), F32),
        grid=(batch, nt),
        in_specs=[col(0), col(1), col(2),
                  pl.BlockSpec((ts, GATE_COLS), lambda b, s: (b * nt + s, 0))],
        out_specs=pl.BlockSpec((ts, MIX_W), lambda b, s: (b * nt + s, 0)),
        scratch_shapes=[pltpu.VMEM((MIX_HEADS, HEAD_DIM, HEAD_DIM), F32)],
        compiler_params=pltpu.CompilerParams(dimension_semantics=("parallel", "arbitrary"),
                                             vmem_limit_bytes=VMEM_LIMIT),
        name="gdn",
    )(qkv, qkv, qkv, gates)


def _mlstm_kernel(q_ref, k_ref, v_ref, g_ref, o_ref, c_ref, m_ref, *, ts):
    @pl.when(pl.program_id(1) == 0)
    def _():
        c_ref[...] = jnp.zeros_like(c_ref)
        m_ref[...] = jnp.zeros_like(m_ref)

    L = CHUNK
    g = g_ref[...]
    cs = _chunk_cumsum(g, ts)
    lane = lax.broadcasted_iota(jnp.int32, (L, GATE_COLS), 1)
    r = lax.broadcasted_iota(jnp.int32, (L, L), 0)
    c = lax.broadcasted_iota(jnp.int32, (L, L), 1)
    causal = r >= c
    ones = jnp.ones((L, HEAD_DIM), F32)
    m_cur = [m_ref[h][0:1, 0:1] for h in range(MIX_HEADS)]

    for ch in range(ts // L):
        rows = slice(ch * L, (ch + 1) * L)
        gch = g[rows]
        csch = cs[rows]
        for h in range(MIX_HEADS):
            cols = slice(h * HEAD_DIM, (h + 1) * HEAD_DIM)
            li_lane = 8 + h
            f_lane = 12 + h
            q16 = q_ref[rows, cols].astype(BF16)
            k = k_ref[rows, cols]
            v = v_ref[rows, cols]
            e_f = (lane == f_lane).astype(F32)
            e_if = jnp.logical_or(lane == li_lane, lane == f_lane).astype(F32)
            xm = jnp.concatenate([_lane_pick(csch, lane, f_lane), e_if], axis=1)
            ym = jnp.concatenate([e_f, _lane_pick(gch, lane, li_lane) - _lane_pick(csch, lane, f_lane)],
                                 axis=1)
            dmat = jnp.where(causal, _dot_nt(xm, ym, precision=HIGHEST), NEG_BIG)
            rowmax = jnp.max(dmat, axis=1, keepdims=True)
            bc = csch[:, f_lane:f_lane + 1]
            li = gch[:, li_lane:li_lane + 1]
            bl = csch[L - 1:L, f_lane:f_lane + 1]
            dl = bl - bc + li
            dlmax = jnp.max(dl, axis=0, keepdims=True)
            m = m_cur[h]
            inter = bc + m
            mt = jnp.maximum(inter, rowmax)
            p = _dot_nt(q16, k.astype(BF16)) * jnp.exp(dmat - mt)
            sc = jnp.exp(inter - mt)
            vaug = jnp.concatenate([v, ones], axis=1).astype(BF16)
            cst = c_ref[h]
            nd = sc * _dot(q16, cst.astype(BF16)) + _dot(p.astype(BF16), vaug)
            den = jnp.maximum(jnp.abs(nd[:, HEAD_DIM:]), jnp.exp(-mt))
            o_ref[rows, cols] = nd[:, :HEAD_DIM] / den
            mnew = jnp.maximum(bl + m, dlmax)
            wgt = jnp.exp(dl - mnew)
            c_ref[h] = jnp.exp(bl + m - mnew) * cst + _dot_tn((k * wgt).astype(BF16), vaug)
            m_cur[h] = mnew

    for h in range(MIX_HEADS):
        m_ref[h] = jnp.broadcast_to(m_cur[h], m_ref.shape[1:])


def _mlstm(qkv, zvo, gates, *, batch, seq, ts):
    t = qkv.shape[0]
    nt = seq // ts
    kern = functools.partial(_mlstm_kernel, ts=ts)

    def col(j):
        return pl.BlockSpec((ts, MIX_W), lambda b, s, j=j: (b * nt + s, j))

    return pl.pallas_call(
        kern,
        out_shape=jax.ShapeDtypeStruct((t, MIX_W), F32),
        grid=(batch, nt),
        in_specs=[col(3), col(4), col(1),
                  pl.BlockSpec((ts, GATE_COLS), lambda b, s: (b * nt + s, 0))],
        out_specs=pl.BlockSpec((ts, MIX_W), lambda b, s: (b * nt + s, 0)),
        scratch_shapes=[pltpu.VMEM((MIX_HEADS, HEAD_DIM, 2 * HEAD_DIM), F32),
                        pltpu.VMEM((MIX_HEADS, 8, 128), F32)],
        compiler_params=pltpu.CompilerParams(dimension_semantics=("parallel", "arbitrary"),
                                             vmem_limit_bytes=VMEM_LIMIT),
        name="mlstm",
    )(qkv, qkv, zvo, gates)


def _mix_xattn_kernel(x_ref, go_ref, mh_ref, z_ref, mo_ref, gnw_ref, mnw_ref, wout_ref,
                      nxa_ref, wq_ref, k_ref, v_ref, wo_ref, out_ref):
    parts = []
    gnw = gnw_ref[...]
    for h in range(MIX_HEADS):
        cols = slice(h * HEAD_DIM, (h + 1) * HEAD_DIM)
        z = z_ref[:, cols]
        parts.append(_rms(go_ref[:, cols], gnw) * (z * _sigmoid(z)))
    for h in range(MIX_HEADS):
        cols = slice(h * HEAD_DIM, (h + 1) * HEAD_DIM)
        parts.append(_rms(mh_ref[:, cols], mnw_ref[:, cols]) * _sigmoid(mo_ref[:, cols]))
    mixed = jnp.concatenate(parts, axis=1).astype(BF16)
    x1 = x_ref[...] + _dot(mixed, wout_ref[...])

    xn = _rms(x1, nxa_ref[...]).astype(BF16)
    q = _dot(xn, wq_ref[...])
    kk = k_ref[0]
    vv = v_ref[0]
    outs = []
    for h in range(XA_HEADS):
        cols = slice(h * XA_HEAD_DIM, (h + 1) * XA_HEAD_DIM)
        s = _dot_nt(q[:, cols].astype(BF16), kk[:, cols]) * (XA_HEAD_DIM ** -0.5)
        s = s - jnp.max(s, axis=-1, keepdims=True)
        e = jnp.exp(s)
        p = e / jnp.sum(e, axis=-1, keepdims=True)
        outs.append(_dot(p.astype(BF16), vv[:, cols]))
    o = jnp.concatenate(outs, axis=1).astype(BF16)
    out_ref[...] = x1 + _dot(o, wo_ref[...])


def _mix_xattn(x2d, go, mh, zvo, gnw, mnw, wout, nxa, wq, kmem, vmem, wo, *, seq, tm):
    t = x2d.shape[0]
    per_seq = seq // tm

    def row(width, j=0):
        return pl.BlockSpec((tm, width), lambda i, j=j: (i, j))

    def const(shape):
        return pl.BlockSpec(shape, lambda i: (0,) * len(shape))

    mem_spec = pl.BlockSpec((1, MEM_TOKENS, D_MODEL), lambda i: (i // per_seq, 0, 0))
    return pl.pallas_call(
        _mix_xattn_kernel,
        out_shape=jax.ShapeDtypeStruct((t, D_MODEL), F32),
        grid=(t // tm,),
        in_specs=[row(D_MODEL), row(MIX_W), row(MIX_W), row(MIX_W, 0), row(MIX_W, 2),
                  const((1, HEAD_DIM)), const((1, MIX_W)), const((D_MODEL, D_MODEL)),
                  const((1, D_MODEL)), const((D_MODEL, D_MODEL)), mem_spec, mem_spec,
                  const((D_MODEL, D_MODEL))],
        out_specs=row(D_MODEL),
        compiler_params=pltpu.CompilerParams(dimension_semantics=("parallel",),
                                             vmem_limit_bytes=VMEM_LIMIT),
        name="mix_xattn",
    )(x2d, go, mh, zvo, zvo, gnw, mnw, wout, nxa, wq, kmem, vmem, wo)


def _top16_ranks(scores):
    nk, tt = scores.shape
    kidx = lax.broadcasted_iota(jnp.int32, (nk, tt), 0).astype(F32)
    slot = lax.broadcasted_iota(jnp.int32, (PEER_TOPK, tt), 0)
    v = scores
    rank = jnp.full((nk, tt), float(PEER_TOPK), F32)
    vals = jnp.zeros((PEER_TOPK, tt), F32)
    for kk in range(PEER_TOPK):
        m = jnp.max(v, axis=0, keepdims=True)
        first = jnp.min(jnp.where(v == m, kidx, float(nk)), axis=0, keepdims=True)
        sel = kidx == first
        rank = jnp.where(sel, float(kk), rank)
        v = jnp.where(sel, -jnp.inf, v)
        vals = jnp.where(slot == kk, m, vals)
    return rank, vals


_CAND_GROUPS = ((0, None, 0), (0, None, 8), (1, None, 0), (2, None, 0),
                (3, 4, 0), (5, 6, 0), (7, 8, 0), (9, 10, 0), (11, 12, 0), (13, 14, 0),
                (15, None, 0))


def _route_kernel(x_ref, nw_ref, wq_ref, keys_ref, xn_ref, c0_ref, ap_ref, r1_ref, bp_ref,
                  xn_scr, *, tt):
    @pl.when(pl.program_id(1) == 0)
    def _():
        xn_scr[...] = _rms(x_ref[...], nw_ref[...]).astype(BF16)

    xn = xn_scr[...]
    xn_ref[...] = xn
    q = _dot(xn, wq_ref[...])
    s0 = _dot_nt(keys_ref[0, 0], q[:, :PEER_HALF].astype(BF16))
    s1 = _dot_nt(keys_ref[0, 1], q[:, PEER_HALF:].astype(BF16))
    r0, a = _top16_ranks(s0)
    r1, b = _top16_ranks(s1)

    sub = lax.broadcasted_iota(jnp.int32, (8, 1), 0)
    b_lo4 = jnp.where(sub < 4, b[0:8], pltpu.roll(b[0:8], 4, 0))
    cands, poss, valids, prow = [], [], [], []
    for (p0, p1, q0) in _CAND_GROUPS:
        if p1 is None:
            av = jnp.broadcast_to(a[p0:p0 + 1], (8, tt))
            bv = b[q0:q0 + 8]
            pr = jnp.full((8, 1), p0, jnp.int32)
            qr = sub + q0
        else:
            av = jnp.where(sub < 4, a[p0:p0 + 1], a[p1:p1 + 1])
            bv = b_lo4
            pr = jnp.where(sub < 4, p0, p1)
            qr = sub % 4
        cands.append(av + bv)
        poss.append((pr * PEER_TOPK + qr).astype(F32))
        valids.append((pr + 1) * (qr + 1) <= PEER_TOPK)
        prow.append(pr)
    cand = jnp.concatenate(cands, axis=0)
    pos = jnp.broadcast_to(jnp.concatenate(poss, axis=0), cand.shape)
    valid = jnp.concatenate(valids, axis=0)
    pidx = jnp.concatenate(prow, axis=0)
    cv = jnp.where(valid, cand, -jnp.inf)
    chosen = jnp.zeros(cand.shape, F32)
    for _ in range(PEER_TOPK):
        m = jnp.max(cv, axis=0, keepdims=True)
        first = jnp.min(jnp.where(cv == m, pos, 1e9), axis=0, keepdims=True)
        sel = pos == first
        chosen = jnp.where(sel, 1.0, chosen)
        cv = jnp.where(sel, -jnp.inf, cv)

    top = a[0:1] + b[0:1]
    gsel = chosen * jnp.exp(jnp.where(valid, cand, top) - top)
    zinv = 1.0 / jnp.sum(gsel, axis=0, keepdims=True)

    c0 = jnp.zeros(r0.shape, F32)
    for p in range(PEER_TOPK):
        cnt = jnp.sum(jnp.where(pidx == p, chosen, 0.0), axis=0, keepdims=True)
        c0 = jnp.where(r0 == float(p), cnt, c0)

    c0_ref[0] = c0
    ap_ref[0] = jnp.exp(s0 - a[0:1]) * zinv
    r1_ref[0] = r1.astype(BF16)
    bp_ref[0] = jnp.exp(s1 - b[0:1]).astype(BF16)


def _peer_route(x2, nw, wq, keys, *, tt):
    t = x2.shape[0]
    kern = functools.partial(_route_kernel, tt=tt)
    tab = lambda dt: jax.ShapeDtypeStruct((PEER_HEADS, PEER_KEYS, t), dt)
    tab_spec = pl.BlockSpec((1, PEER_KEYS, tt), lambda i, h: (h, 0, i))
    return pl.pallas_call(
        kern,
        out_shape=(jax.ShapeDtypeStruct((t, D_MODEL), BF16), tab(F32), tab(F32), tab(BF16), tab(BF16)),
        grid=(t // tt, PEER_HEADS),
        in_specs=[pl.BlockSpec((tt, D_MODEL), lambda i, h: (i, 0)),
                  pl.BlockSpec((1, D_MODEL), lambda i, h: (0, 0)),
                  pl.BlockSpec((D_MODEL, 2 * PEER_HALF), lambda i, h: (0, h)),
                  pl.BlockSpec((1, 2, PEER_KEYS, PEER_HALF), lambda i, h: (h, 0, 0, 0))],
        out_specs=(pl.BlockSpec((tt, D_MODEL), lambda i, h: (i, 0)),
                   tab_spec, tab_spec, tab_spec, tab_spec),
        scratch_shapes=[pltpu.VMEM((tt, D_MODEL), BF16)],
        compiler_params=pltpu.CompilerParams(dimension_semantics=("parallel", "arbitrary"),
                                             vmem_limit_bytes=VMEM_LIMIT),
        name="peer_route",
    )(x2, nw, wq, keys)


def _peer_eval_kernel(xn_ref, x_ref, u_ref, v_ref, c0_ref, ap_ref, r1_ref, bp_ref, fnw_ref,
                      out_ref, acc_ref, *, eb, sb):
    e = pl.program_id(1)

    @pl.when(e == 0)
    def _():
        acc_ref[...] = jnp.zeros_like(acc_ref)

    xn = xn_ref[...]
    inv_sqrt2 = 1.0 / math.sqrt(2.0)
    for s in range(eb // sb):
        rows = slice(s * sb, (s + 1) * sb)
        act = _dot_nt(u_ref[rows, :], xn)
        gel = 0.5 * act * (1.0 + lax.erf(act * inv_sqrt2))
        ws = []
        for ii in range(sb // PEER_KEYS):
            il = s * (sb // PEER_KEYS) + ii
            w = None
            for h in range(PEER_HEADS):
                c0 = c0_ref[h, il:il + 1, :].astype(BF16)
                ap = ap_ref[h, il:il + 1, :].astype(BF16)
                term = jnp.where(r1_ref[h] < c0, ap * bp_ref[h], jnp.zeros((), BF16))
                w = term if w is None else w + term
            ws.append(w)
        wgt = jnp.concatenate(ws, axis=0)
        pt = (wgt.astype(F32) * gel).astype(BF16)
        acc_ref[...] += _dot_tn(pt, v_ref[rows, :])

    @pl.when(e == pl.num_programs(1) - 1)
    def _():
        out_ref[...] = _rms(x_ref[...] + acc_ref[...], fnw_ref[...])


def _peer_eval(xn, x2, u16, v16, c0, ap, r1, bp, fnw, *, tt, eb, sb):
    t = x2.shape[0]
    ne = u16.shape[0]
    ib = eb // PEER_KEYS
    kern = functools.partial(_peer_eval_kernel, eb=eb, sb=sb)
    half0 = pl.BlockSpec((PEER_HEADS, ib, tt), lambda i, e: (0, e, i))
    half1 = pl.BlockSpec((PEER_HEADS, PEER_KEYS, tt), lambda i, e: (0, 0, i))
    return pl.pallas_call(
        kern,
        out_shape=jax.ShapeDtypeStruct((t, D_MODEL), F32),
        grid=(t // tt, ne // eb),
        in_specs=[pl.BlockSpec((tt, D_MODEL), lambda i, e: (i, 0)),
                  pl.BlockSpec((tt, D_MODEL), lambda i, e: (i, 0)),
                  pl.BlockSpec((eb, D_MODEL), lambda i, e: (e, 0)),
                  pl.BlockSpec((eb, D_MODEL), lambda i, e: (e, 0)),
                  half0, half0, half1, half1,
                  pl.BlockSpec((1, D_MODEL), lambda i, e: (0, 0))],
        out_specs=pl.BlockSpec((tt, D_MODEL), lambda i, e: (i, 0)),
        scratch_shapes=[pltpu.VMEM((tt, D_MODEL), F32)],
        compiler_params=pltpu.CompilerParams(dimension_semantics=("parallel", "arbitrary"),
                                             vmem_limit_bytes=VMEM_LIMIT),
        name="peer_eval",
    )(xn, x2, u16, v16, c0, ap, r1, bp, fnw)


def _gate_row(vals_by_offset):
    row = jnp.zeros((1, GATE_COLS), F32)
    for off, val in vals_by_offset:
        row = row.at[0, off:off + val.shape[0]].set(val.astype(F32))
    return row


def kernel(x, mem, norm_mix_w, w_in, gdn_conv_w, gdn_a_log, gdn_dt_bias, gdn_norm_w, mlstm_conv_w,
           mlstm_i_bias, mlstm_f_bias, mlstm_norm_w, w_out, norm_xa_w, norm_mem_w, xa_wq, xa_wkv,
           xa_wo, norm_ffn_w, peer_wq, peer_sub_keys, peer_u, peer_v, norm_final_w):
    batch, seq, d = x.shape
    t = batch * seq
    assert d == D_MODEL and w_in.shape[0] == 1, "single-layer block; the final norm is fused into it"
    tm = min(256, seq)
    ts = min(256, seq)
    tt = min(512, t)
    xs = x.reshape(t, d)
    mem2d = mem.reshape(batch * MEM_TOKENS, d)

    wl = w_in[0]
    o_gz = 3 * MIX_W
    o_ga = o_gz + MIX_W
    o_gb = o_ga + MIX_HEADS
    o_mqk = o_gb + MIX_HEADS
    o_mv = o_mqk + 2 * MIX_W
    o_mo = o_mv + MIX_W
    o_mi = o_mo + MIX_W
    o_mf = o_mi + MIX_HEADS
    gate_w = jnp.zeros((d, GATE_COLS), wl.dtype)
    gate_w = gate_w.at[:, 0:4].set(wl[:, o_ga:o_gb]).at[:, 4:8].set(wl[:, o_gb:o_mqk])
    gate_w = gate_w.at[:, 8:12].set(wl[:, o_mi:o_mf]).at[:, 12:16].set(wl[:, o_mf:o_mf + MIX_HEADS])
    w_all = jnp.concatenate([wl[:, 0:o_gz], wl[:, o_mqk:o_mv], wl[:, o_gz:o_ga],
                             wl[:, o_mv:o_mo], wl[:, o_mo:o_mi], gate_w], axis=1).astype(BF16)
    cw = jnp.concatenate([gdn_conv_w[0], mlstm_conv_w[0]], axis=1).astype(F32)
    gbias = _gate_row([(0, gdn_dt_bias[0]), (8, mlstm_i_bias[0]), (12, mlstm_f_bias[0])])
    alog = _gate_row([(0, gdn_a_log[0])])

    kmem, vmem = _memkv(mem2d, norm_mem_w[0].reshape(1, d), xa_wkv[0].astype(BF16))
    qkv, zvo, gates = _inproj(xs, norm_mix_w[0].reshape(1, d), w_all, cw, gbias, alog,
                              seq=seq, tm=tm)
    go = _gdn(qkv, gates, batch=batch, seq=seq, ts=ts)
    mh = _mlstm(qkv, zvo, gates, batch=batch, seq=seq, ts=ts)
    x2 = _mix_xattn(xs, go, mh, zvo, gdn_norm_w[0].reshape(1, HEAD_DIM),
                    mlstm_norm_w[0].reshape(1, MIX_W), w_out[0].astype(BF16),
                    norm_xa_w[0].reshape(1, d), xa_wq[0].astype(BF16),
                    kmem.reshape(batch, MEM_TOKENS, d), vmem.reshape(batch, MEM_TOKENS, d),
                    xa_wo[0].astype(BF16), seq=seq, tm=tm)
    xn, c0, ap, r1, bp = _peer_route(x2, norm_ffn_w[0].reshape(1, d), peer_wq[0].astype(BF16),
                                     peer_sub_keys[0].astype(BF16), tt=tt)
    out = _peer_eval(xn, x2, peer_u[0].astype(BF16), peer_v[0].astype(BF16), c0, ap, r1, bp,
                     norm_final_w.reshape(1, d), tt=tt, eb=1024, sb=512)
    return out.reshape(batch, seq, d)
```

```python
import functools
import math

import jax
import jax.numpy as jnp
from jax import lax
from jax.experimental import pallas as pl
from jax.experimental.pallas import tpu as pltpu

F32 = jnp.float32
BF16 = jnp.bfloat16
HIGHEST = lax.Precision.HIGHEST

D_MODEL = 1024
HEAD_DIM = 128
MIX_HEADS = 4
MIX_W = MIX_HEADS * HEAD_DIM
CONV_WIDTH = 4
CHUNK = 64
MEM_TOKENS = 256
XA_HEADS = 4
XA_HEAD_DIM = D_MODEL // XA_HEADS
PEER_HEADS = 8
PEER_KEYS = 128
PEER_TOPK = 16
PEER_HALF = 128
NORM_EPS = 1e-6
NEG_BIG = -1e30
LANES = 128
MXU_COLS = 256
BF16_ROWS = 16
RANK_SCALE = 2.0

CONV_COLS = 3 * MIX_W + 2 * MIX_W
PLAIN_COLS = 3 * MIX_W
GATE_COLS = 128
PROJ_COLS = CONV_COLS + PLAIN_COLS + GATE_COLS
HALO = 8

VMEM_LIMIT = 56 * 1024 * 1024

NT_DIMS = (((1,), (1,)), ((), ()))
TN_DIMS = (((0,), (0,)), ((), ()))


def _rms(x, w):
    return x * lax.rsqrt(jnp.mean(x * x, axis=-1, keepdims=True) + NORM_EPS) * w


def _sigmoid(x):
    return 1.0 / (1.0 + jnp.exp(-x))


def _softplus(x):
    return jnp.maximum(x, 0.0) + jnp.log1p(jnp.exp(-jnp.abs(x)))


def _dot(a, b):
    return jnp.dot(a, b, preferred_element_type=F32)


def _dot_nt(a, b, precision=None):
    return lax.dot_general(a, b, NT_DIMS, precision=precision, preferred_element_type=F32)


def _dot_tn(a, b):
    return lax.dot_general(a, b, TN_DIMS, preferred_element_type=F32)


def _memkv_kernel(m_ref, nw_ref, w_ref, k_ref, v_ref):
    mn = _rms(m_ref[...], nw_ref[...]).astype(BF16)
    kv = _dot(mn, w_ref[...])
    k_ref[...] = kv[:, :D_MODEL].astype(BF16)
    v_ref[...] = kv[:, D_MODEL:].astype(BF16)


def _memkv(mem2d, nw, wkv):
    n = mem2d.shape[0]
    tm = MEM_TOKENS
    return pl.pallas_call(
        _memkv_kernel,
        out_shape=(jax.ShapeDtypeStruct((n, D_MODEL), BF16),
                   jax.ShapeDtypeStruct((n, D_MODEL), BF16)),
        grid=(n // tm,),
        in_specs=[pl.BlockSpec((tm, D_MODEL), lambda i: (i, 0)),
                  pl.BlockSpec((1, D_MODEL), lambda i: (0, 0)),
                  pl.BlockSpec((D_MODEL, 2 * D_MODEL), lambda i: (0, 0))],
        out_specs=(pl.BlockSpec((tm, D_MODEL), lambda i: (i, 0)),
                   pl.BlockSpec((tm, D_MODEL), lambda i: (i, 0))),
        compiler_params=pltpu.CompilerParams(dimension_semantics=("parallel",),
                                             vmem_limit_bytes=VMEM_LIMIT),
        name="memkv",
    )(mem2d, nw, wkv)


def _inproj_kernel(x_ref, halo_ref, nw_ref, w_ref, cw_ref, gbias_ref, alog_ref,
                   qkv_ref, zvo_ref, gate_ref, p_scr, *, tm, seq):
    i = pl.program_id(0)
    xc = jnp.concatenate([halo_ref[...], x_ref[...]], axis=0)
    xn = _rms(xc, nw_ref[...]).astype(BF16)
    p = _dot(xn, w_ref[...])

    first_row = jnp.where((i * tm) % seq == 0, HALO, 0)
    row = lax.broadcasted_iota(jnp.int32, (HALO + tm, 1), 0)
    p_scr[...] = jnp.where(row >= first_row, p[:, :CONV_COLS], 0.0)
    cw = cw_ref[...]
    y = cw[CONV_WIDTH - 1:CONV_WIDTH] * p_scr[HALO:HALO + tm, :]
    for kk in range(CONV_WIDTH - 1):
        off = HALO - (CONV_WIDTH - 1) + kk
        y = y + cw[kk:kk + 1] * p_scr[off:off + tm, :]
    y = y * _sigmoid(y)

    scale = HEAD_DIM ** -0.5
    for hh in range(2 * MIX_HEADS):
        cols = slice(hh * HEAD_DIM, (hh + 1) * HEAD_DIM)
        yh = y[:, cols]
        inv = lax.rsqrt(jnp.sum(yh * yh, axis=-1, keepdims=True) + NORM_EPS)
        if hh < MIX_HEADS:
            inv = inv * scale
        qkv_ref[:, cols] = yh * inv
    qkv_ref[:, 2 * MIX_W:4 * MIX_W] = y[:, 2 * MIX_W:4 * MIX_W]
    qkv_ref[:, 4 * MIX_W:5 * MIX_W] = y[:, 4 * MIX_W:5 * MIX_W] * scale

    zvo_ref[...] = p[HALO:, CONV_COLS:CONV_COLS + PLAIN_COLS]

    ga = p[HALO:, CONV_COLS + PLAIN_COLS:] + gbias_ref[...]
    lane = lax.broadcasted_iota(jnp.int32, (tm, GATE_COLS), 1)
    log_alpha = -jnp.exp(alog_ref[...]) * _softplus(ga)
    beta = _sigmoid(ga)
    log_f = -_softplus(-ga)
    gate_ref[...] = jnp.where(lane < 4, log_alpha,
                              jnp.where(lane < 8, beta,
                                        jnp.where(lane < 12, ga,
                                                  jnp.where(lane < 16, log_f, 0.0))))


def _inproj(x2d, nw, w_all, cw, gbias, alog, *, seq, tm):
    t = x2d.shape[0]
    hb = tm // HALO
    kern = functools.partial(_inproj_kernel, tm=tm, seq=seq)
    return pl.pallas_call(
        kern,
        out_shape=(jax.ShapeDtypeStruct((t, CONV_COLS), F32),
                   jax.ShapeDtypeStruct((t, PLAIN_COLS), F32),
                   jax.ShapeDtypeStruct((t, GATE_COLS), F32)),
        grid=(t // tm,),
        in_specs=[pl.BlockSpec((tm, D_MODEL), lambda i: (i, 0)),
                  pl.BlockSpec((HALO, D_MODEL), lambda i: (jnp.maximum(i * hb - 1, 0), 0)),
                  pl.BlockSpec((1, D_MODEL), lambda i: (0, 0)),
                  pl.BlockSpec((D_MODEL, PROJ_COLS), lambda i: (0, 0)),
                  pl.BlockSpec((CONV_WIDTH, CONV_COLS), lambda i: (0, 0)),
                  pl.BlockSpec((1, GATE_COLS), lambda i: (0, 0)),
                  pl.BlockSpec((1, GATE_COLS), lambda i: (0, 0))],
        out_specs=(pl.BlockSpec((tm, CONV_COLS), lambda i: (i, 0)),
                   pl.BlockSpec((tm, PLAIN_COLS), lambda i: (i, 0)),
                   pl.BlockSpec((tm, GATE_COLS), lambda i: (i, 0))),
        scratch_shapes=[pltpu.VMEM((HALO + tm, CONV_COLS), F32)],
        compiler_params=pltpu.CompilerParams(dimension_semantics=("parallel",),
                                             vmem_limit_bytes=VMEM_LIMIT),
        name="inproj",
    )(x2d, x2d, nw, w_all, cw, gbias, alog)


def _chunk_cumsum(g, ts):
    ri = lax.broadcasted_iota(jnp.int32, (ts, ts), 0)
    ci = lax.broadcasted_iota(jnp.int32, (ts, ts), 1)
    shift = CHUNK.bit_length() - 1
    same_chunk = jnp.right_shift(ri, shift) == jnp.right_shift(ci, shift)
    tri = jnp.where(jnp.logical_and(same_chunk, ci <= ri), 1.0, 0.0)
    return jnp.dot(tri, g, precision=HIGHEST, preferred_element_type=F32)


def _lane_pick(x, lane, idx):
    return jnp.where(lane == idx, x, 0.0)


def _gdn_kernel(q_ref, k_ref, v_ref, g_ref, o_ref, s_ref, *, ts):
    @pl.when(pl.program_id(1) == 0)
    def _():
        s_ref[...] = jnp.zeros_like(s_ref)

    L = CHUNK
    nch = ts // L
    g = g_ref[...]
    cs = _chunk_cumsum(g, ts)
    lane = lax.broadcasted_iota(jnp.int32, (L, GATE_COLS), 1)
    r = lax.broadcasted_iota(jnp.int32, (L, L), 0)
    c = lax.broadcasted_iota(jnp.int32, (L, L), 1)
    causal = r >= c
    strict = r > c
    eye = jnp.where(r == c, 1.0, 0.0)

    items = [(ch, h) for ch in range(nch) for h in range(MIX_HEADS)]
    n = len(items)

    def rows_of(ch):
        return slice(ch * L, (ch + 1) * L)

    def cols_of(h):
        return slice(h * HEAD_DIM, (h + 1) * HEAD_DIM)

    q = [q_ref[rows_of(ch), cols_of(h)] for ch, h in items]
    k = [k_ref[rows_of(ch), cols_of(h)] for ch, h in items]
    v = [v_ref[rows_of(ch), cols_of(h)] for ch, h in items]
    gc = [cs[rows_of(ch), h:h + 1] for ch, h in items]
    beta = [g[rows_of(ch), 4 + h:5 + h] for ch, h in items]
    gl = [cs[(ch + 1) * L - 1:(ch + 1) * L, h:h + 1] for ch, h in items]

    diff = []
    for ch, h in items:
        e_h = jnp.where(lane == h, 1.0, 0.0)
        cs_h = _lane_pick(cs[rows_of(ch)], lane, h)
        diff.append(_dot_nt(jnp.concatenate([cs_h, e_h], axis=1),
                            jnp.concatenate([e_h, -cs_h], axis=1), precision=HIGHEST))
    decay = [jnp.exp(jnp.where(causal, d, NEG_BIG)) for d in diff]
    eg = [jnp.exp(x) for x in gc]
    kb = [k[i] * beta[i] for i in range(n)]
    k16 = [x.astype(BF16) for x in k]
    q16 = [x.astype(BF16) for x in q]
    kk = [_dot_nt(kb[i].astype(BF16), k16[i]) for i in range(n)]
    qk = [_dot_nt(q16[i], k16[i]) for i in range(n)]
    a = [jnp.where(strict, kk[i] * decay[i], 0.0) for i in range(n)]
    qk16 = [(qk[i] * decay[i]).astype(BF16) for i in range(n)]
    tinv = [eye - x for x in a]
    pw = a
    for _ in range(5):
        pw16 = [x.astype(BF16) for x in pw]
        pw = [_dot(x, x) for x in pw16]
        tinv = [tinv[i] + _dot(tinv[i].astype(BF16), pw[i].astype(BF16)) for i in range(n)]
    rhs = [jnp.concatenate([v[i] * beta[i], kb[i] * eg[i]], axis=1).astype(BF16) for i in range(n)]
    uw = [_dot(tinv[i].astype(BF16), rhs[i]) for i in range(n)]
    wq16 = [jnp.concatenate([uw[i][:, HEAD_DIM:], q[i] * eg[i]], axis=0).astype(BF16)
            for i in range(n)]
    kd16 = [(k[i] * jnp.exp(gl[i] - gc[i])).astype(BF16) for i in range(n)]
    egl = [jnp.exp(x) for x in gl]

    state = [s_ref[h] for h in range(MIX_HEADS)]
    for ch in range(nch):
        base = ch * MIX_HEADS
        rr = [_dot(wq16[base + h], state[h].astype(BF16)) for h in range(MIX_HEADS)]
        vnew16 = [(uw[base + h][:, :HEAD_DIM] - rr[h][:L]).astype(BF16) for h in range(MIX_HEADS)]
        for h in range(MIX_HEADS):
            o_ref[rows_of(ch), cols_of(h)] = rr[h][L:] + _dot(qk16[base + h], vnew16[h])
        state = [state[h] * egl[base + h] + _dot_tn(kd16[base + h], vnew16[h])
                 for h in range(MIX_HEADS)]
    for h in range(MIX_HEADS):
        s_ref[h] = state[h]


def _gdn(qkv, gates, *, batch, seq, ts):
    t = qkv.shape[0]
    nt = seq // ts
    kern = functools.partial(_gdn_kernel, ts=ts)

    def col(j):
        return pl.BlockSpec((ts, MIX_W), lambda b, s, j=j: (b * nt + s, j))

    return pl.pallas_call(
        kern,
        out_shape=jax.ShapeDtypeStruct((t, MIX_W), F32),
        grid=(batch, nt),
        in_specs=[col(0), col(1), col(2),
                  pl.BlockSpec((ts, GATE_COLS), lambda b, s: (b * nt + s, 0))],
        out_specs=pl.BlockSpec((ts, MIX_W), lambda b, s: (b * nt + s, 0)),
        scratch_shapes=[pltpu.VMEM((MIX_HEADS, HEAD_DIM, HEAD_DIM), F32)],
        compiler_params=pltpu.CompilerParams(dimension_semantics=("parallel", "arbitrary"),
                                             vmem_limit_bytes=VMEM_LIMIT),
        name="gdn",
    )(qkv, qkv, qkv, gates)


def _mlstm_kernel(q_ref, k_ref, v_ref, g_ref, o_ref, c_ref, m_ref, *, ts):
    @pl.when(pl.program_id(1) == 0)
    def _():
        c_ref[...] = jnp.zeros_like(c_ref)
        m_ref[...] = jnp.zeros_like(m_ref)

    L = CHUNK
    nch = ts // L
    g = g_ref[...]
    cs = _chunk_cumsum(g, ts)
    lane = lax.broadcasted_iota(jnp.int32, (L, GATE_COLS), 1)
    r = lax.broadcasted_iota(jnp.int32, (L, L), 0)
    c = lax.broadcasted_iota(jnp.int32, (L, L), 1)
    causal = r >= c
    ones = jnp.ones((L, HEAD_DIM), F32)

    items = [(ch, h) for ch in range(nch) for h in range(MIX_HEADS)]
    n = len(items)

    def rows_of(ch):
        return slice(ch * L, (ch + 1) * L)

    def cols_of(h):
        return slice(h * HEAD_DIM, (h + 1) * HEAD_DIM)

    q16 = [q_ref[rows_of(ch), cols_of(h)].astype(BF16) for ch, h in items]
    k = [k_ref[rows_of(ch), cols_of(h)] for ch, h in items]
    vaug = [jnp.concatenate([v_ref[rows_of(ch), cols_of(h)], ones], axis=1).astype(BF16)
            for ch, h in items]
    bc = [cs[rows_of(ch), 12 + h:13 + h] for ch, h in items]
    li = [g[rows_of(ch), 8 + h:9 + h] for ch, h in items]
    bl = [cs[(ch + 1) * L - 1:(ch + 1) * L, 12 + h:13 + h] for ch, h in items]
    dmat = []
    for ch, h in items:
        csch = cs[rows_of(ch)]
        gch = g[rows_of(ch)]
        e_f = jnp.where(lane == 12 + h, 1.0, 0.0)
        e_if = jnp.where(jnp.logical_or(lane == 8 + h, lane == 12 + h), 1.0, 0.0)
        xm = jnp.concatenate([_lane_pick(csch, lane, 12 + h), e_if], axis=1)
        ym = jnp.concatenate([e_f, _lane_pick(gch, lane, 8 + h) - _lane_pick(csch, lane, 12 + h)],
                             axis=1)
        dmat.append(jnp.where(causal, _dot_nt(xm, ym, precision=HIGHEST), NEG_BIG))
    qk = [_dot_nt(q16[i], k[i].astype(BF16)) for i in range(n)]
    rowmax = [jnp.max(d, axis=1, keepdims=True) for d in dmat]
    dl = [bl[i] - bc[i] + li[i] for i in range(n)]
    dlmax = [jnp.max(x, axis=0, keepdims=True) for x in dl]

    m_in = []
    m_cur = [m_ref[h][0:1, 0:1] for h in range(MIX_HEADS)]
    for i, (ch, h) in enumerate(items):
        m_in.append(m_cur[h])
        m_cur[h] = jnp.maximum(bl[i] + m_cur[h], dlmax[i])
    m_out = [m_cur[h] if ch == nch - 1 else m_in[(ch + 1) * MIX_HEADS + h] for ch, h in items]
    for h in range(MIX_HEADS):
        m_ref[h] = jnp.broadcast_to(m_cur[h], m_ref.shape[1:])

    inter = [bc[i] + m_in[i] for i in range(n)]
    mt = [jnp.maximum(inter[i], rowmax[i]) for i in range(n)]
    p16 = [(qk[i] * jnp.exp(dmat[i] - mt[i])).astype(BF16) for i in range(n)]
    sc = [jnp.exp(inter[i] - mt[i]) for i in range(n)]
    pv = [_dot(p16[i], vaug[i]) for i in range(n)]
    kw16 = [(k[i] * jnp.exp(dl[i] - m_out[i])).astype(BF16) for i in range(n)]
    kv = [_dot_tn(kw16[i], vaug[i]) for i in range(n)]
    dec = [jnp.exp(bl[i] + m_in[i] - m_out[i]) for i in range(n)]

    cst = [c_ref[h] for h in range(MIX_HEADS)]
    c_in = []
    for i, (ch, h) in enumerate(items):
        c_in.append(cst[h].astype(BF16))
        cst[h] = dec[i] * cst[h] + kv[i]
    for h in range(MIX_HEADS):
        c_ref[h] = cst[h]

    for i, (ch, h) in enumerate(items):
        nd = sc[i] * _dot(q16[i], c_in[i]) + pv[i]
        den = jnp.maximum(jnp.abs(nd[:, HEAD_DIM:]), jnp.exp(-mt[i]))
        o_ref[rows_of(ch), cols_of(h)] = nd[:, :HEAD_DIM] / den


def _mlstm(qkv, zvo, gates, *, batch, seq, ts):
    t = qkv.shape[0]
    nt = seq // ts
    kern = functools.partial(_mlstm_kernel, ts=ts)

    def col(j):
        return pl.BlockSpec((ts, MIX_W), lambda b, s, j=j: (b * nt + s, j))

    return pl.pallas_call(
        kern,
        out_shape=jax.ShapeDtypeStruct((t, MIX_W), F32),
        grid=(batch, nt),
        in_specs=[col(3), col(4), col(1),
                  pl.BlockSpec((ts, GATE_COLS), lambda b, s: (b * nt + s, 0))],
        out_specs=pl.BlockSpec((ts, MIX_W), lambda b, s: (b * nt + s, 0)),
        scratch_shapes=[pltpu.VMEM((MIX_HEADS, HEAD_DIM, 2 * HEAD_DIM), F32),
                        pltpu.VMEM((MIX_HEADS, 8, 128), F32)],
        compiler_params=pltpu.CompilerParams(dimension_semantics=("parallel", "arbitrary"),
                                             vmem_limit_bytes=VMEM_LIMIT),
        name="mlstm",
    )(qkv, qkv, zvo, gates)


def _mix_xattn_kernel(x_ref, go_ref, mh_ref, z_ref, mo_ref, gnw_ref, mnw_ref, wout_ref,
                      nxa_ref, wq_ref, k_ref, v_ref, wo_ref, out_ref):
    parts = []
    gnw = gnw_ref[...]
    for h in range(MIX_HEADS):
        cols = slice(h * HEAD_DIM, (h + 1) * HEAD_DIM)
        z = z_ref[:, cols]
        parts.append(_rms(go_ref[:, cols], gnw) * (z * _sigmoid(z)))
    for h in range(MIX_HEADS):
        cols = slice(h * HEAD_DIM, (h + 1) * HEAD_DIM)
        parts.append(_rms(mh_ref[:, cols], mnw_ref[:, cols]) * _sigmoid(mo_ref[:, cols]))
    mixed = jnp.concatenate(parts, axis=1).astype(BF16)
    x1 = x_ref[...] + _dot(mixed, wout_ref[...])

    xn = _rms(x1, nxa_ref[...]).astype(BF16)
    q = _dot(xn, wq_ref[...])
    kk = k_ref[0]
    vv = v_ref[0]
    outs = []
    for h in range(XA_HEADS):
        cols = slice(h * XA_HEAD_DIM, (h + 1) * XA_HEAD_DIM)
        s = _dot_nt(q[:, cols].astype(BF16), kk[:, cols]) * (XA_HEAD_DIM ** -0.5)
        s = s - jnp.max(s, axis=-1, keepdims=True)
        e = jnp.exp(s)
        p = e / jnp.sum(e, axis=-1, keepdims=True)
        outs.append(_dot(p.astype(BF16), vv[:, cols]))
    o = jnp.concatenate(outs, axis=1).astype(BF16)
    out_ref[...] = x1 + _dot(o, wo_ref[...])


def _mix_xattn(x2d, go, mh, zvo, gnw, mnw, wout, nxa, wq, kmem, vmem, wo, *, seq, tm):
    t = x2d.shape[0]
    per_seq = seq // tm

    def row(width, j=0):
        return pl.BlockSpec((tm, width), lambda i, j=j: (i, j))

    def const(shape):
        return pl.BlockSpec(shape, lambda i: (0,) * len(shape))

    mem_spec = pl.BlockSpec((1, MEM_TOKENS, D_MODEL), lambda i: (i // per_seq, 0, 0))
    return pl.pallas_call(
        _mix_xattn_kernel,
        out_shape=jax.ShapeDtypeStruct((t, D_MODEL), F32),
        grid=(t // tm,),
        in_specs=[row(D_MODEL), row(MIX_W), row(MIX_W), row(MIX_W, 0), row(MIX_W, 2),
                  const((1, HEAD_DIM)), const((1, MIX_W)), const((D_MODEL, D_MODEL)),
                  const((1, D_MODEL)), const((D_MODEL, D_MODEL)), mem_spec, mem_spec,
                  const((D_MODEL, D_MODEL))],
        out_specs=row(D_MODEL),
        compiler_params=pltpu.CompilerParams(dimension_semantics=("parallel",),
                                             vmem_limit_bytes=VMEM_LIMIT),
        name="mix_xattn",
    )(x2d, go, mh, zvo, zvo, gnw, mnw, wout, nxa, wq, kmem, vmem, wo)


def _take_top16(values, order, exact):
    tt = values.shape[1]
    slot = lax.broadcasted_iota(jnp.int32, (PEER_TOPK, tt), 0)
    v = values
    step = jnp.full(values.shape, float(PEER_TOPK), F32)
    vals = jnp.zeros((PEER_TOPK, tt), F32)
    for kk in range(PEER_TOPK):
        m = jnp.max(v, axis=0, keepdims=True)
        if exact:
            first = jnp.min(jnp.where(v == m, order, 1e9), axis=0, keepdims=True)
            sel = order == first
        else:
            sel = v == m
        step = jnp.where(sel, float(kk), step)
        v = jnp.where(sel, -jnp.inf, v)
        vals = jnp.where(slot == kk, m, vals)
    count = jnp.sum(jnp.where(step < float(PEER_TOPK), 1.0, 0.0), axis=0, keepdims=True)
    return step, vals, count


_CAND_GROUPS = ((0, None, 0), (0, None, 8), (1, None, 0), (2, None, 0),
                (3, 4, 0), (5, 6, 0), (7, 8, 0), (9, 10, 0), (11, 12, 0), (13, 14, 0),
                (15, None, 0))


def _route_kernel(x_ref, nw_ref, wq_ref, keys_ref, xn_ref, c0_ref, ap_ref, r1_ref, bp_ref,
                  xn_scr, *, tt):
    @pl.when(pl.program_id(1) == 0)
    def _():
        xn_scr[...] = _rms(x_ref[...], nw_ref[...]).astype(BF16)

    xn = xn_scr[...]
    xn_ref[...] = xn
    q = _dot(xn, wq_ref[...])
    s0 = _dot_nt(keys_ref[0, 0], q[:, :PEER_HALF].astype(BF16))
    s1 = _dot_nt(keys_ref[0, 1], q[:, PEER_HALF:].astype(BF16))

    def tables(exact):
        kidx = lax.broadcasted_iota(jnp.int32, s0.shape, 0).astype(F32)
        r0, a, n0 = _take_top16(s0, kidx, exact)
        r1, b, n1 = _take_top16(s1, kidx, exact)

        sub = lax.broadcasted_iota(jnp.int32, (8, 1), 0)
        b_lo4 = jnp.where(sub < 4, b[0:8], pltpu.roll(b[0:8], 4, 0))
        cands, poss, valids, prow = [], [], [], []
        for (p0, p1, q0) in _CAND_GROUPS:
            if p1 is None:
                av = jnp.broadcast_to(a[p0:p0 + 1], (8, tt))
                bv = b[q0:q0 + 8]
                pr = jnp.full((8, 1), p0, jnp.int32)
                qr = sub + q0
            else:
                av = jnp.where(sub < 4, a[p0:p0 + 1], a[p1:p1 + 1])
                bv = b_lo4
                pr = jnp.where(sub < 4, p0, p1)
                qr = sub % 4
            cands.append(av + bv)
            poss.append((pr * PEER_TOPK + qr).astype(F32))
            valids.append((pr + 1) * (qr + 1) <= PEER_TOPK)
            prow.append(pr)
        cand = jnp.concatenate(cands, axis=0)
        pos = jnp.broadcast_to(jnp.concatenate(poss, axis=0), cand.shape)
        valid = jnp.concatenate(valids, axis=0)
        pidx = jnp.concatenate(prow, axis=0)
        taken, _, n2 = _take_top16(jnp.where(valid, cand, -jnp.inf), pos, exact)
        chosen = jnp.where(taken < float(PEER_TOPK), 1.0, 0.0)

        top = a[0:1] + b[0:1]
        gsel = chosen * jnp.exp(jnp.where(valid, cand, top) - top)
        zinv = 1.0 / jnp.sum(gsel, axis=0, keepdims=True)

        c0 = jnp.zeros(r0.shape, F32)
        for p in range(PEER_TOPK):
            cnt = jnp.sum(jnp.where(pidx == p, chosen, 0.0), axis=0, keepdims=True)
            c0 = jnp.where(r0 == float(p), cnt, c0)

        c0_ref[0] = c0
        ap_ref[0] = jnp.exp(s0 - a[0:1]) * (zinv * math.sqrt(0.5))
        r1_ref[0] = (r1 * RANK_SCALE).astype(BF16)
        bp_ref[0] = jnp.exp(s1 - b[0:1]).astype(BF16)
        return n0 + n1 + n2

    taken_total = tables(exact=False)
    tied = jnp.max(jnp.abs(taken_total - 3.0 * PEER_TOPK)) > 0.0

    @pl.when(tied)
    def _():
        tables(exact=True)


def _peer_route(x2, nw, wq, keys, *, tt):
    t = x2.shape[0]
    kern = functools.partial(_route_kernel, tt=tt)
    tab = lambda dt: jax.ShapeDtypeStruct((PEER_HEADS, PEER_KEYS, t), dt)
    tab_spec = pl.BlockSpec((1, PEER_KEYS, tt), lambda i, h: (h, 0, i))
    return pl.pallas_call(
        kern,
        out_shape=(jax.ShapeDtypeStruct((t, D_MODEL), BF16), tab(F32), tab(F32), tab(BF16), tab(BF16)),
        grid=(t // tt, PEER_HEADS),
        in_specs=[pl.BlockSpec((tt, D_MODEL), lambda i, h: (i, 0)),
                  pl.BlockSpec((1, D_MODEL), lambda i, h: (0, 0)),
                  pl.BlockSpec((D_MODEL, 2 * PEER_HALF), lambda i, h: (0, h)),
                  pl.BlockSpec((1, 2, PEER_KEYS, PEER_HALF), lambda i, h: (h, 0, 0, 0))],
        out_specs=(pl.BlockSpec((tt, D_MODEL), lambda i, h: (i, 0)),
                   tab_spec, tab_spec, tab_spec, tab_spec),
        scratch_shapes=[pltpu.VMEM((tt, D_MODEL), BF16)],
        compiler_params=pltpu.CompilerParams(dimension_semantics=("parallel", "arbitrary"),
                                             vmem_limit_bytes=VMEM_LIMIT),
        name="peer_route",
    )(x2, nw, wq, keys)


def _peer_eval_kernel(xn_ref, x_ref, u_ref, v_ref, c0_ref, ap_ref, r1_ref, bp_ref, fnw_ref,
                      out_ref, acc_ref, w_scr, *, eb, sb):
    e = pl.program_id(1)

    @pl.when(e == 0)
    def _():
        acc_ref[...] = jnp.zeros_like(acc_ref)

    xn = xn_ref[...]
    tt = xn.shape[0]
    inv_sqrt2 = 1.0 / math.sqrt(2.0)
    nsub = eb // sb
    ipb = sb // PEER_KEYS

    def activations(s):
        return _dot_nt(u_ref[s * sb:(s + 1) * sb, :], xn)

    def row_tile(ref, h, il, lanes, scale):
        row = ref[h, il:il + 1, lanes] * scale
        return jnp.broadcast_to(row, (BF16_ROWS, LANES)).astype(BF16)[None]

    def weight_slab(s, j):
        ii, lg = divmod(j, tt // LANES)
        il = s * ipb + ii
        lanes = slice(lg * LANES, (lg + 1) * LANES)
        w = None
        for h in range(PEER_HEADS):
            gate = jnp.clip(row_tile(c0_ref, h, il, lanes, RANK_SCALE) - r1_ref[h, :, :, lanes],
                            0.0, row_tile(ap_ref, h, il, lanes, 1.0))
            term = gate * bp_ref[h, :, :, lanes]
            w = term if w is None else w + term
        w_scr[ii * PEER_KEYS:(ii + 1) * PEER_KEYS, lanes] = w.reshape(PEER_KEYS, LANES)

    act = activations(0)
    part = None
    for s in range(nsub):
        act_next = activations(s + 1) if s + 1 < nsub else None
        for j in range(ipb * (tt // LANES)):
            weight_slab(s, j)
        z = (act * inv_sqrt2).astype(BF16)
        pt = w_scr[...] * (z * (1.0 + lax.erf(z)))
        d = _dot_tn(pt, v_ref[s * sb:(s + 1) * sb, :])
        part = d if part is None else part + d
        act = act_next
    acc_ref[...] += part

    @pl.when(e == pl.num_programs(1) - 1)
    def _():
        out_ref[...] = _rms(x_ref[...] + acc_ref[...], fnw_ref[...])


def _peer_eval(xn, x2, u16, v16, c0, ap, r1, bp, fnw, *, tt, eb, sb):
    t = x2.shape[0]
    ne = u16.shape[0]
    ib = eb // PEER_KEYS
    kern = functools.partial(_peer_eval_kernel, eb=eb, sb=sb)
    half0 = pl.BlockSpec((PEER_HEADS, ib, tt), lambda i, e: (0, e, i))
    ktiles = PEER_KEYS // BF16_ROWS
    r1 = r1.reshape(PEER_HEADS, ktiles, BF16_ROWS, t)
    bp = bp.reshape(PEER_HEADS, ktiles, BF16_ROWS, t)
    half1 = pl.BlockSpec((PEER_HEADS, ktiles, BF16_ROWS, tt), lambda i, e: (0, 0, 0, i))
    return pl.pallas_call(
        kern,
        out_shape=jax.ShapeDtypeStruct((t, D_MODEL), F32),
        grid=(t // tt, ne // eb),
        in_specs=[pl.BlockSpec((tt, D_MODEL), lambda i, e: (i, 0)),
                  pl.BlockSpec((tt, D_MODEL), lambda i, e: (i, 0)),
                  pl.BlockSpec((eb, D_MODEL), lambda i, e: (e, 0)),
                  pl.BlockSpec((eb, D_MODEL), lambda i, e: (e, 0)),
                  half0, half0, half1, half1,
                  pl.BlockSpec((1, D_MODEL), lambda i, e: (0, 0))],
        out_specs=pl.BlockSpec((tt, D_MODEL), lambda i, e: (i, 0)),
        scratch_shapes=[pltpu.VMEM((tt, D_MODEL), F32), pltpu.VMEM((sb, tt), BF16)],
        compiler_params=pltpu.CompilerParams(dimension_semantics=("parallel", "arbitrary"),
                                             vmem_limit_bytes=VMEM_LIMIT),
        name="peer_eval",
    )(xn, x2, u16, v16, c0, ap, r1, bp, fnw)


def _gate_row(vals_by_offset):
    row = jnp.zeros((1, GATE_COLS), F32)
    for off, val in vals_by_offset:
        row = row.at[0, off:off + val.shape[0]].set(val.astype(F32))
    return row


def kernel(x, mem, norm_mix_w, w_in, gdn_conv_w, gdn_a_log, gdn_dt_bias, gdn_norm_w, mlstm_conv_w,
           mlstm_i_bias, mlstm_f_bias, mlstm_norm_w, w_out, norm_xa_w, norm_mem_w, xa_wq, xa_wkv,
           xa_wo, norm_ffn_w, peer_wq, peer_sub_keys, peer_u, peer_v, norm_final_w):
    batch, seq, d = x.shape
    t = batch * seq
    assert d == D_MODEL and w_in.shape[0] == 1, "single-layer block; the final norm is fused into it"
    tm = min(256, seq)
    ts = min(256, seq)
    tt = min(512, t)
    xs = x.reshape(t, d)
    mem2d = mem.reshape(batch * MEM_TOKENS, d)

    wl = w_in[0]
    o_gz = 3 * MIX_W
    o_ga = o_gz + MIX_W
    o_gb = o_ga + MIX_HEADS
    o_mqk = o_gb + MIX_HEADS
    o_mv = o_mqk + 2 * MIX_W
    o_mo = o_mv + MIX_W
    o_mi = o_mo + MIX_W
    o_mf = o_mi + MIX_HEADS
    gate_w = jnp.zeros((d, GATE_COLS), wl.dtype)
    gate_w = gate_w.at[:, 0:4].set(wl[:, o_ga:o_gb]).at[:, 4:8].set(wl[:, o_gb:o_mqk])
    gate_w = gate_w.at[:, 8:12].set(wl[:, o_mi:o_mf]).at[:, 12:16].set(wl[:, o_mf:o_mf + MIX_HEADS])
    w_all = jnp.concatenate([wl[:, 0:o_gz], wl[:, o_mqk:o_mv], wl[:, o_gz:o_ga],
                             wl[:, o_mv:o_mo], wl[:, o_mo:o_mi], gate_w], axis=1).astype(BF16)
    cw = jnp.concatenate([gdn_conv_w[0], mlstm_conv_w[0]], axis=1).astype(F32)
    gbias = _gate_row([(0, gdn_dt_bias[0]), (8, mlstm_i_bias[0]), (12, mlstm_f_bias[0])])
    alog = _gate_row([(0, gdn_a_log[0])])

    kmem, vmem = _memkv(mem2d, norm_mem_w[0].reshape(1, d), xa_wkv[0].astype(BF16))
    qkv, zvo, gates = _inproj(xs, norm_mix_w[0].reshape(1, d), w_all, cw, gbias, alog,
                              seq=seq, tm=tm)
    go = _gdn(qkv, gates, batch=batch, seq=seq, ts=ts)
    mh = _mlstm(qkv, zvo, gates, batch=batch, seq=seq, ts=ts)
    x2 = _mix_xattn(xs, go, mh, zvo, gdn_norm_w[0].reshape(1, HEAD_DIM),
                    mlstm_norm_w[0].reshape(1, MIX_W), w_out[0].astype(BF16),
                    norm_xa_w[0].reshape(1, d), xa_wq[0].astype(BF16),
                    kmem.reshape(batch, MEM_TOKENS, d), vmem.reshape(batch, MEM_TOKENS, d),
                    xa_wo[0].astype(BF16), seq=seq, tm=tm)
    xn, c0, ap, r1, bp = _peer_route(x2, norm_ffn_w[0].reshape(1, d), peer_wq[0].astype(BF16),
                                     peer_sub_keys[0].astype(BF16), tt=tt)
    out = _peer_eval(xn, x2, peer_u[0].astype(BF16), peer_v[0].astype(BF16), c0, ap, r1, bp,
                     norm_final_w.reshape(1, d), tt=tt, eb=2048, sb=512)
    return out.reshape(batch, seq, d)
```

```python
import functools
import math

import jax
import jax.numpy as jnp
from jax import lax
from jax.experimental import pallas as pl
from jax.experimental.pallas import tpu as pltpu

F32 = jnp.float32
BF16 = jnp.bfloat16
HIGHEST = lax.Precision.HIGHEST

D_MODEL = 1024
HEAD_DIM = 128
MIX_HEADS = 4
MIX_W = MIX_HEADS * HEAD_DIM
CONV_WIDTH = 4
CHUNK = 64
MEM_TOKENS = 256
XA_HEADS = 4
XA_HEAD_DIM = D_MODEL // XA_HEADS
PEER_HEADS = 8
PEER_KEYS = 128
PEER_TOPK = 16
PEER_HALF = 128
NORM_EPS = 1e-6
NEG_BIG = -1e30
LANES = 128
MXU_COLS = 256
BF16_ROWS = 16
RANK_SCALE = 2.0

CONV_COLS = 3 * MIX_W + 2 * MIX_W
PLAIN_COLS = 3 * MIX_W
GATE_COLS = 128
PROJ_COLS = CONV_COLS + PLAIN_COLS + GATE_COLS
HALO = 8

VMEM_LIMIT = 60 * 1024 * 1024

NT_DIMS = (((1,), (1,)), ((), ()))
TN_DIMS = (((0,), (0,)), ((), ()))


def _rms(x, w):
    return x * lax.rsqrt(jnp.mean(x * x, axis=-1, keepdims=True) + NORM_EPS) * w


def _sigmoid(x):
    return 1.0 / (1.0 + jnp.exp(-x))


def _softplus(x):
    return jnp.maximum(x, 0.0) + jnp.log1p(jnp.exp(-jnp.abs(x)))


def _dot(a, b):
    return jnp.dot(a, b, preferred_element_type=F32)


def _dot_nt(a, b, precision=None):
    return lax.dot_general(a, b, NT_DIMS, precision=precision, preferred_element_type=F32)


def _dot_tn(a, b):
    return lax.dot_general(a, b, TN_DIMS, preferred_element_type=F32)


def _memkv_kernel(m_ref, nw_ref, w_ref, k_ref, v_ref):
    mn = _rms(m_ref[...], nw_ref[...]).astype(BF16)
    kv = _dot(mn, w_ref[...])
    k_ref[...] = kv[:, :D_MODEL].astype(BF16)
    v_ref[...] = kv[:, D_MODEL:].astype(BF16)


def _memkv(mem2d, nw, wkv):
    n = mem2d.shape[0]
    tm = MEM_TOKENS
    return pl.pallas_call(
        _memkv_kernel,
        out_shape=(jax.ShapeDtypeStruct((n, D_MODEL), BF16),
                   jax.ShapeDtypeStruct((n, D_MODEL), BF16)),
        grid=(n // tm,),
        in_specs=[pl.BlockSpec((tm, D_MODEL), lambda i: (i, 0)),
                  pl.BlockSpec((1, D_MODEL), lambda i: (0, 0)),
                  pl.BlockSpec((D_MODEL, 2 * D_MODEL), lambda i: (0, 0))],
        out_specs=(pl.BlockSpec((tm, D_MODEL), lambda i: (i, 0)),
                   pl.BlockSpec((tm, D_MODEL), lambda i: (i, 0))),
        compiler_params=pltpu.CompilerParams(dimension_semantics=("parallel",),
                                             vmem_limit_bytes=VMEM_LIMIT),
        name="memkv",
    )(mem2d, nw, wkv)


def _inproj_kernel(x_ref, halo_ref, nw_ref, w_ref, cw_ref, gbias_ref, alog_ref,
                   qkv_ref, zvo_ref, gate_ref, p_scr, *, tm, seq):
    i = pl.program_id(0)
    xc = jnp.concatenate([halo_ref[...], x_ref[...]], axis=0)
    xn = _rms(xc, nw_ref[...]).astype(BF16)
    p = _dot(xn, w_ref[...])

    first_row = jnp.where((i * tm) % seq == 0, HALO, 0)
    row = lax.broadcasted_iota(jnp.int32, (HALO, 1), 0)
    p_scr[:HALO, :] = jnp.where(row >= first_row, p[:HALO, :CONV_COLS], 0.0)
    p_scr[HALO:, :] = p[HALO:, :CONV_COLS]
    cw = cw_ref[...]
    y = cw[CONV_WIDTH - 1:CONV_WIDTH] * p_scr[HALO:HALO + tm, :]
    for kk in range(CONV_WIDTH - 1):
        off = HALO - (CONV_WIDTH - 1) + kk
        y = y + cw[kk:kk + 1] * p_scr[off:off + tm, :]
    y = y * _sigmoid(y)

    scale = HEAD_DIM ** -0.5
    for hh in range(2 * MIX_HEADS):
        cols = slice(hh * HEAD_DIM, (hh + 1) * HEAD_DIM)
        yh = y[:, cols]
        inv = lax.rsqrt(jnp.sum(yh * yh, axis=-1, keepdims=True) + NORM_EPS)
        if hh < MIX_HEADS:
            inv = inv * scale
        qkv_ref[:, cols] = yh * inv
    qkv_ref[:, 2 * MIX_W:4 * MIX_W] = y[:, 2 * MIX_W:4 * MIX_W]
    qkv_ref[:, 4 * MIX_W:5 * MIX_W] = y[:, 4 * MIX_W:5 * MIX_W] * scale

    zvo_ref[...] = p[HALO:, CONV_COLS:CONV_COLS + PLAIN_COLS]

    ga = p[HALO:, CONV_COLS + PLAIN_COLS:] + gbias_ref[...]
    lane = lax.broadcasted_iota(jnp.int32, (tm, GATE_COLS), 1)
    log_alpha = -jnp.exp(alog_ref[...]) * _softplus(ga)
    beta = _sigmoid(ga)
    log_f = -_softplus(-ga)
    gate_ref[...] = jnp.where(lane < 4, log_alpha,
                              jnp.where(lane < 8, beta,
                                        jnp.where(lane < 12, ga,
                                                  jnp.where(lane < 16, log_f, 0.0))))


def _inproj(x2d, nw, w_all, cw, gbias, alog, *, seq, tm):
    t = x2d.shape[0]
    hb = tm // HALO
    kern = functools.partial(_inproj_kernel, tm=tm, seq=seq)
    return pl.pallas_call(
        kern,
        out_shape=(jax.ShapeDtypeStruct((t, CONV_COLS), F32),
                   jax.ShapeDtypeStruct((t, PLAIN_COLS), F32),
                   jax.ShapeDtypeStruct((t, GATE_COLS), F32)),
        grid=(t // tm,),
        in_specs=[pl.BlockSpec((tm, D_MODEL), lambda i: (i, 0)),
                  pl.BlockSpec((HALO, D_MODEL), lambda i: (jnp.maximum(i * hb - 1, 0), 0)),
                  pl.BlockSpec((1, D_MODEL), lambda i: (0, 0)),
                  pl.BlockSpec((D_MODEL, PROJ_COLS), lambda i: (0, 0)),
                  pl.BlockSpec((CONV_WIDTH, CONV_COLS), lambda i: (0, 0)),
                  pl.BlockSpec((1, GATE_COLS), lambda i: (0, 0)),
                  pl.BlockSpec((1, GATE_COLS), lambda i: (0, 0))],
        out_specs=(pl.BlockSpec((tm, CONV_COLS), lambda i: (i, 0)),
                   pl.BlockSpec((tm, PLAIN_COLS), lambda i: (i, 0)),
                   pl.BlockSpec((tm, GATE_COLS), lambda i: (i, 0))),
        scratch_shapes=[pltpu.VMEM((HALO + tm, CONV_COLS), F32)],
        compiler_params=pltpu.CompilerParams(dimension_semantics=("parallel",),
                                             vmem_limit_bytes=VMEM_LIMIT),
        name="inproj",
    )(x2d, x2d, nw, w_all, cw, gbias, alog)


def _chunk_cumsum(g, ts):
    ri = lax.broadcasted_iota(jnp.int32, (ts, ts), 0)
    ci = lax.broadcasted_iota(jnp.int32, (ts, ts), 1)
    shift = CHUNK.bit_length() - 1
    same_chunk = jnp.right_shift(ri, shift) == jnp.right_shift(ci, shift)
    tri = jnp.where(jnp.logical_and(same_chunk, ci <= ri), 1.0, 0.0)
    return jnp.dot(tri, g, precision=HIGHEST, preferred_element_type=F32)


def _lane_pick(x, lane, idx):
    return jnp.where(lane == idx, x, 0.0)


def _gdn_kernel(q_ref, k_ref, v_ref, g_ref, o_ref, s_ref, *, ts):
    @pl.when(pl.program_id(1) == 0)
    def _():
        s_ref[...] = jnp.zeros_like(s_ref)

    L = CHUNK
    nch = ts // L
    g = g_ref[...]
    cs = _chunk_cumsum(g, ts)
    lane = lax.broadcasted_iota(jnp.int32, (L, GATE_COLS), 1)
    r = lax.broadcasted_iota(jnp.int32, (L, L), 0)
    c = lax.broadcasted_iota(jnp.int32, (L, L), 1)
    causal = r >= c
    strict = r > c
    eye = jnp.where(r == c, 1.0, 0.0)

    items = [(ch, h) for ch in range(nch) for h in range(MIX_HEADS)]
    n = len(items)

    def rows_of(ch):
        return slice(ch * L, (ch + 1) * L)

    def cols_of(h):
        return slice(h * HEAD_DIM, (h + 1) * HEAD_DIM)

    q = [q_ref[rows_of(ch), cols_of(h)] for ch, h in items]
    k = [k_ref[rows_of(ch), cols_of(h)] for ch, h in items]
    v = [v_ref[rows_of(ch), cols_of(h)] for ch, h in items]
    gc = [cs[rows_of(ch), h:h + 1] for ch, h in items]
    beta = [g[rows_of(ch), 4 + h:5 + h] for ch, h in items]
    gl = [cs[(ch + 1) * L - 1:(ch + 1) * L, h:h + 1] for ch, h in items]

    diff = []
    for ch, h in items:
        e_h = jnp.where(lane == h, 1.0, 0.0)
        cs_h = _lane_pick(cs[rows_of(ch)], lane, h)
        diff.append(_dot_nt(jnp.concatenate([cs_h, e_h], axis=1),
                            jnp.concatenate([e_h, -cs_h], axis=1), precision=HIGHEST))
    decay = [jnp.exp(jnp.where(causal, d, NEG_BIG)) for d in diff]
    eg = [jnp.exp(x) for x in gc]
    kb = [k[i] * beta[i] for i in range(n)]
    k16 = [x.astype(BF16) for x in k]
    q16 = [x.astype(BF16) for x in q]
    kk = [_dot_nt(kb[i].astype(BF16), k16[i]) for i in range(n)]
    qk = [_dot_nt(q16[i], k16[i]) for i in range(n)]
    a = [jnp.where(strict, kk[i] * decay[i], 0.0) for i in range(n)]
    qk16 = [(qk[i] * decay[i]).astype(BF16) for i in range(n)]
    tinv = [eye - x for x in a]
    pw = a
    for _ in range(5):
        pw16 = [x.astype(BF16) for x in pw]
        pw = [_dot(x, x) for x in pw16]
        tinv = [tinv[i] + _dot(tinv[i].astype(BF16), pw[i].astype(BF16)) for i in range(n)]
    rhs = [jnp.concatenate([v[i] * beta[i], kb[i] * eg[i]], axis=1).astype(BF16) for i in range(n)]
    uw = [_dot(tinv[i].astype(BF16), rhs[i]) for i in range(n)]
    wq16 = [jnp.concatenate([uw[i][:, HEAD_DIM:], q[i] * eg[i]], axis=0).astype(BF16)
            for i in range(n)]
    kd16 = [(k[i] * jnp.exp(gl[i] - gc[i])).astype(BF16) for i in range(n)]
    egl = [jnp.exp(x) for x in gl]

    state = [s_ref[h] for h in range(MIX_HEADS)]
    for ch in range(nch):
        base = ch * MIX_HEADS
        rr = [_dot(wq16[base + h], state[h].astype(BF16)) for h in range(MIX_HEADS)]
        vnew16 = [(uw[base + h][:, :HEAD_DIM] - rr[h][:L]).astype(BF16) for h in range(MIX_HEADS)]
        for h in range(MIX_HEADS):
            o_ref[rows_of(ch), cols_of(h)] = rr[h][L:] + _dot(qk16[base + h], vnew16[h])
        state = [state[h] * egl[base + h] + _dot_tn(kd16[base + h], vnew16[h])
                 for h in range(MIX_HEADS)]
    for h in range(MIX_HEADS):
        s_ref[h] = state[h]


def _gdn(qkv, gates, *, batch, seq, ts):
    t = qkv.shape[0]
    nt = seq // ts
    kern = functools.partial(_gdn_kernel, ts=ts)

    def col(j):
        return pl.BlockSpec((ts, MIX_W), lambda b, s, j=j: (b * nt + s, j))

    return pl.pallas_call(
        kern,
        out_shape=jax.ShapeDtypeStruct((t, MIX_W), F32),
        grid=(batch, nt),
        in_specs=[col(0), col(1), col(2),
                  pl.BlockSpec((ts, GATE_COLS), lambda b, s: (b * nt + s, 0))],
        out_specs=pl.BlockSpec((ts, MIX_W), lambda b, s: (b * nt + s, 0)),
        scratch_shapes=[pltpu.VMEM((MIX_HEADS, HEAD_DIM, HEAD_DIM), F32)],
        compiler_params=pltpu.CompilerParams(dimension_semantics=("parallel", "arbitrary"),
                                             vmem_limit_bytes=VMEM_LIMIT),
        name="gdn",
    )(qkv, qkv, qkv, gates)


def _mlstm_kernel(q_ref, k_ref, v_ref, g_ref, o_ref, c_ref, m_ref, *, ts):
    @pl.when(pl.program_id(1) == 0)
    def _():
        c_ref[...] = jnp.zeros_like(c_ref)
        m_ref[...] = jnp.zeros_like(m_ref)

    L = CHUNK
    nch = ts // L
    g = g_ref[...]
    cs = _chunk_cumsum(g, ts)
    lane = lax.broadcasted_iota(jnp.int32, (L, GATE_COLS), 1)
    r = lax.broadcasted_iota(jnp.int32, (L, L), 0)
    c = lax.broadcasted_iota(jnp.int32, (L, L), 1)
    causal = r >= c
    ones = jnp.ones((L, HEAD_DIM), F32)

    items = [(ch, h) for ch in range(nch) for h in range(MIX_HEADS)]
    n = len(items)

    def rows_of(ch):
        return slice(ch * L, (ch + 1) * L)

    def cols_of(h):
        return slice(h * HEAD_DIM, (h + 1) * HEAD_DIM)

    q16 = [q_ref[rows_of(ch), cols_of(h)].astype(BF16) for ch, h in items]
    k = [k_ref[rows_of(ch), cols_of(h)] for ch, h in items]
    vaug = [jnp.concatenate([v_ref[rows_of(ch), cols_of(h)], ones], axis=1).astype(BF16)
            for ch, h in items]
    bc = [cs[rows_of(ch), 12 + h:13 + h] for ch, h in items]
    li = [g[rows_of(ch), 8 + h:9 + h] for ch, h in items]
    bl = [cs[(ch + 1) * L - 1:(ch + 1) * L, 12 + h:13 + h] for ch, h in items]
    dmat = []
    for ch, h in items:
        csch = cs[rows_of(ch)]
        gch = g[rows_of(ch)]
        e_f = jnp.where(lane == 12 + h, 1.0, 0.0)
        e_if = jnp.where(jnp.logical_or(lane == 8 + h, lane == 12 + h), 1.0, 0.0)
        xm = jnp.concatenate([_lane_pick(csch, lane, 12 + h), e_if], axis=1)
        ym = jnp.concatenate([e_f, _lane_pick(gch, lane, 8 + h) - _lane_pick(csch, lane, 12 + h)],
                             axis=1)
        dmat.append(jnp.where(causal, _dot_nt(xm, ym, precision=HIGHEST), NEG_BIG))
    qk = [_dot_nt(q16[i], k[i].astype(BF16)) for i in range(n)]
    rowmax = [jnp.max(d, axis=1, keepdims=True) for d in dmat]
    dl = [bl[i] - bc[i] + li[i] for i in range(n)]
    dlmax = [jnp.max(x, axis=0, keepdims=True) for x in dl]

    m_in = []
    m_cur = [m_ref[h][0:1, 0:1] for h in range(MIX_HEADS)]
    for i, (ch, h) in enumerate(items):
        m_in.append(m_cur[h])
        m_cur[h] = jnp.maximum(bl[i] + m_cur[h], dlmax[i])
    m_out = [m_cur[h] if ch == nch - 1 else m_in[(ch + 1) * MIX_HEADS + h] for ch, h in items]
    for h in range(MIX_HEADS):
        m_ref[h] = jnp.broadcast_to(m_cur[h], m_ref.shape[1:])

    inter = [bc[i] + m_in[i] for i in range(n)]
    mt = [jnp.maximum(inter[i], rowmax[i]) for i in range(n)]
    p16 = [(qk[i] * jnp.exp(dmat[i] - mt[i])).astype(BF16) for i in range(n)]
    sc = [jnp.exp(inter[i] - mt[i]) for i in range(n)]
    pv = [_dot(p16[i], vaug[i]) for i in range(n)]
    kw16 = [(k[i] * jnp.exp(dl[i] - m_out[i])).astype(BF16) for i in range(n)]
    kv = [_dot_tn(kw16[i], vaug[i]) for i in range(n)]
    dec = [jnp.exp(bl[i] + m_in[i] - m_out[i]) for i in range(n)]

    cst = [c_ref[h] for h in range(MIX_HEADS)]
    c_in = []
    for i, (ch, h) in enumerate(items):
        c_in.append(cst[h].astype(BF16))
        cst[h] = dec[i] * cst[h] + kv[i]
    for h in range(MIX_HEADS):
        c_ref[h] = cst[h]

    for i, (ch, h) in enumerate(items):
        nd = sc[i] * _dot(q16[i], c_in[i]) + pv[i]
        den = jnp.maximum(jnp.abs(nd[:, HEAD_DIM:]), jnp.exp(-mt[i]))
        o_ref[rows_of(ch), cols_of(h)] = nd[:, :HEAD_DIM] / den


def _mlstm(qkv, zvo, gates, *, batch, seq, ts):
    t = qkv.shape[0]
    nt = seq // ts
    kern = functools.partial(_mlstm_kernel, ts=ts)

    def col(j):
        return pl.BlockSpec((ts, MIX_W), lambda b, s, j=j: (b * nt + s, j))

    return pl.pallas_call(
        kern,
        out_shape=jax.ShapeDtypeStruct((t, MIX_W), F32),
        grid=(batch, nt),
        in_specs=[col(3), col(4), col(1),
                  pl.BlockSpec((ts, GATE_COLS), lambda b, s: (b * nt + s, 0))],
        out_specs=pl.BlockSpec((ts, MIX_W), lambda b, s: (b * nt + s, 0)),
        scratch_shapes=[pltpu.VMEM((MIX_HEADS, HEAD_DIM, 2 * HEAD_DIM), F32),
                        pltpu.VMEM((MIX_HEADS, 8, 128), F32)],
        compiler_params=pltpu.CompilerParams(dimension_semantics=("parallel", "arbitrary"),
                                             vmem_limit_bytes=VMEM_LIMIT),
        name="mlstm",
    )(qkv, qkv, zvo, gates)


def _mix_xattn_kernel(x_ref, go_ref, mh_ref, z_ref, mo_ref, gnw_ref, mnw_ref, wout_ref,
                      nxa_ref, wq_ref, k_ref, v_ref, wo_ref, out_ref):
    parts = []
    gnw = gnw_ref[...]
    for h in range(MIX_HEADS):
        cols = slice(h * HEAD_DIM, (h + 1) * HEAD_DIM)
        z = z_ref[:, cols]
        parts.append(_rms(go_ref[:, cols], gnw) * (z * _sigmoid(z)))
    for h in range(MIX_HEADS):
        cols = slice(h * HEAD_DIM, (h + 1) * HEAD_DIM)
        parts.append(_rms(mh_ref[:, cols], mnw_ref[:, cols]) * _sigmoid(mo_ref[:, cols]))
    mixed = jnp.concatenate(parts, axis=1).astype(BF16)
    x1 = x_ref[...] + _dot(mixed, wout_ref[...])

    xn = _rms(x1, nxa_ref[...]).astype(BF16)
    q = _dot(xn, wq_ref[...])
    kk = k_ref[0]
    vv = v_ref[0]
    outs = []
    for h in range(XA_HEADS):
        cols = slice(h * XA_HEAD_DIM, (h + 1) * XA_HEAD_DIM)
        s = _dot_nt(q[:, cols].astype(BF16), kk[:, cols]) * (XA_HEAD_DIM ** -0.5)
        s = s - jnp.max(s, axis=-1, keepdims=True)
        e = jnp.exp(s)
        p = e / jnp.sum(e, axis=-1, keepdims=True)
        outs.append(_dot(p.astype(BF16), vv[:, cols]))
    o = jnp.concatenate(outs, axis=1).astype(BF16)
    out_ref[...] = x1 + _dot(o, wo_ref[...])


def _mix_xattn(x2d, go, mh, zvo, gnw, mnw, wout, nxa, wq, kmem, vmem, wo, *, seq, tm):
    t = x2d.shape[0]
    per_seq = seq // tm

    def row(width, j=0):
        return pl.BlockSpec((tm, width), lambda i, j=j: (i, j))

    def const(shape):
        return pl.BlockSpec(shape, lambda i: (0,) * len(shape))

    mem_spec = pl.BlockSpec((1, MEM_TOKENS, D_MODEL), lambda i: (i // per_seq, 0, 0))
    return pl.pallas_call(
        _mix_xattn_kernel,
        out_shape=jax.ShapeDtypeStruct((t, D_MODEL), F32),
        grid=(t // tm,),
        in_specs=[row(D_MODEL), row(MIX_W), row(MIX_W), row(MIX_W, 0), row(MIX_W, 2),
                  const((1, HEAD_DIM)), const((1, MIX_W)), const((D_MODEL, D_MODEL)),
                  const((1, D_MODEL)), const((D_MODEL, D_MODEL)), mem_spec, mem_spec,
                  const((D_MODEL, D_MODEL))],
        out_specs=row(D_MODEL),
        compiler_params=pltpu.CompilerParams(dimension_semantics=("parallel",),
                                             vmem_limit_bytes=VMEM_LIMIT),
        name="mix_xattn",
    )(x2d, go, mh, zvo, zvo, gnw, mnw, wout, nxa, wq, kmem, vmem, wo)


def _take_top16(values, order, exact):
    tt = values.shape[1]
    slot = lax.broadcasted_iota(jnp.int32, (PEER_TOPK, tt), 0)
    v = values
    step = jnp.full(values.shape, float(PEER_TOPK), F32)
    vals = jnp.zeros((PEER_TOPK, tt), F32)
    for kk in range(PEER_TOPK):
        m = jnp.max(v, axis=0, keepdims=True)
        if exact:
            first = jnp.min(jnp.where(v == m, order, 1e9), axis=0, keepdims=True)
            sel = order == first
        else:
            sel = v == m
        step = jnp.where(sel, float(kk), step)
        v = jnp.where(sel, -jnp.inf, v)
        vals = jnp.where(slot == kk, m, vals)
    count = jnp.sum(jnp.where(step < float(PEER_TOPK), 1.0, 0.0), axis=0, keepdims=True)
    return step, vals, count


_CAND_GROUPS = ((0, None, 0), (0, None, 8), (1, None, 0), (2, None, 0),
                (3, 4, 0), (5, 6, 0), (7, 8, 0), (9, 10, 0), (11, 12, 0), (13, 14, 0),
                (15, None, 0))


def _route_kernel(x_ref, nw_ref, wq_ref, keys_ref, xn_ref, c0_ref, ap_ref, r1_ref, bp_ref,
                  xn_scr, *, tt):
    @pl.when(pl.program_id(1) == 0)
    def _():
        xn_scr[...] = _rms(x_ref[...], nw_ref[...]).astype(BF16)

    xn = xn_scr[...]
    xn_ref[...] = xn
    q = _dot(xn, wq_ref[...])
    s0 = _dot_nt(keys_ref[0, 0], q[:, :PEER_HALF].astype(BF16))
    s1 = _dot_nt(keys_ref[0, 1], q[:, PEER_HALF:].astype(BF16))

    def tables(exact):
        kidx = lax.broadcasted_iota(jnp.int32, s0.shape, 0).astype(F32)
        r0, a, n0 = _take_top16(s0, kidx, exact)
        r1, b, n1 = _take_top16(s1, kidx, exact)

        sub = lax.broadcasted_iota(jnp.int32, (8, 1), 0)
        b_lo4 = jnp.where(sub < 4, b[0:8], pltpu.roll(b[0:8], 4, 0))
        cands, poss, valids, prow = [], [], [], []
        for (p0, p1, q0) in _CAND_GROUPS:
            if p1 is None:
                av = jnp.broadcast_to(a[p0:p0 + 1], (8, tt))
                bv = b[q0:q0 + 8]
                pr = jnp.full((8, 1), p0, jnp.int32)
                qr = sub + q0
            else:
                av = jnp.where(sub < 4, a[p0:p0 + 1], a[p1:p1 + 1])
                bv = b_lo4
                pr = jnp.where(sub < 4, p0, p1)
                qr = sub % 4
            cands.append(av + bv)
            poss.append((pr * PEER_TOPK + qr).astype(F32))
            valids.append((pr + 1) * (qr + 1) <= PEER_TOPK)
            prow.append(pr)
        cand = jnp.concatenate(cands, axis=0)
        pos = jnp.broadcast_to(jnp.concatenate(poss, axis=0), cand.shape)
        valid = jnp.concatenate(valids, axis=0)
        pidx = jnp.concatenate(prow, axis=0)
        taken, _, n2 = _take_top16(jnp.where(valid, cand, -jnp.inf), pos, exact)
        chosen = jnp.where(taken < float(PEER_TOPK), 1.0, 0.0)

        top = a[0:1] + b[0:1]
        gsel = chosen * jnp.exp(jnp.where(valid, cand, top) - top)
        zinv = 1.0 / jnp.sum(gsel, axis=0, keepdims=True)

        c0 = jnp.zeros(r0.shape, F32)
        for p in range(PEER_TOPK):
            cnt = jnp.sum(jnp.where(pidx == p, chosen, 0.0), axis=0, keepdims=True)
            c0 = jnp.where(r0 == float(p), cnt, c0)

        c0_ref[0] = c0
        ap_ref[0] = jnp.exp(s0 - a[0:1]) * (zinv * math.sqrt(0.5))
        r1_ref[0] = (r1 * RANK_SCALE).astype(BF16)
        bp_ref[0] = jnp.exp(s1 - b[0:1]).astype(BF16)
        return n0 + n1 + n2

    taken_total = tables(exact=False)
    tied = jnp.max(jnp.abs(taken_total - 3.0 * PEER_TOPK)) > 0.0

    @pl.when(tied)
    def _():
        tables(exact=True)


def _peer_route(x2, nw, wq, keys, *, tt):
    t = x2.shape[0]
    kern = functools.partial(_route_kernel, tt=tt)
    tab = lambda dt: jax.ShapeDtypeStruct((PEER_HEADS, PEER_KEYS, t), dt)
    tab_spec = pl.BlockSpec((1, PEER_KEYS, tt), lambda i, h: (h, 0, i))
    return pl.pallas_call(
        kern,
        out_shape=(jax.ShapeDtypeStruct((t, D_MODEL), BF16), tab(F32), tab(F32), tab(BF16), tab(BF16)),
        grid=(t // tt, PEER_HEADS),
        in_specs=[pl.BlockSpec((tt, D_MODEL), lambda i, h: (i, 0)),
                  pl.BlockSpec((1, D_MODEL), lambda i, h: (0, 0)),
                  pl.BlockSpec((D_MODEL, 2 * PEER_HALF), lambda i, h: (0, h)),
                  pl.BlockSpec((1, 2, PEER_KEYS, PEER_HALF), lambda i, h: (h, 0, 0, 0))],
        out_specs=(pl.BlockSpec((tt, D_MODEL), lambda i, h: (i, 0)),
                   tab_spec, tab_spec, tab_spec, tab_spec),
        scratch_shapes=[pltpu.VMEM((tt, D_MODEL), BF16)],
        compiler_params=pltpu.CompilerParams(dimension_semantics=("parallel", "arbitrary"),
                                             vmem_limit_bytes=VMEM_LIMIT),
        name="peer_route",
    )(x2, nw, wq, keys)


def _peer_eval_kernel(xn_ref, x_ref, u_ref, v_ref, c0_ref, ap_ref, r1_ref, bp_ref, fnw_ref,
                      out_ref, acc_ref, w_scr, *, eb, sb):
    e = pl.program_id(1)

    @pl.when(e == 0)
    def _():
        acc_ref[...] = jnp.zeros_like(acc_ref)

    xn = xn_ref[...]
    tt = xn.shape[0]
    nsub = eb // sb
    ipb = sb // PEER_KEYS

    def activations(s):
        return _dot_nt(u_ref[s * sb:(s + 1) * sb, :], xn)

    def row_tile(ref, h, il, lanes, scale):
        row = ref[h, il:il + 1, lanes] * scale
        return jnp.broadcast_to(row, (BF16_ROWS, LANES)).astype(BF16)[None]

    def weight_slab(s, j):
        ii, lg = divmod(j, tt // LANES)
        il = s * ipb + ii
        lanes = slice(lg * LANES, (lg + 1) * LANES)
        w = None
        for h in range(PEER_HEADS):
            gate = jnp.clip(row_tile(c0_ref, h, il, lanes, RANK_SCALE) - r1_ref[h, :, :, lanes],
                            0.0, row_tile(ap_ref, h, il, lanes, 1.0))
            term = gate * bp_ref[h, :, :, lanes]
            w = term if w is None else w + term
        w_scr[ii * PEER_KEYS:(ii + 1) * PEER_KEYS, lanes] = w.reshape(PEER_KEYS, LANES)

    act = activations(0)
    part = None
    for s in range(nsub):
        act_next = activations(s + 1) if s + 1 < nsub else None
        for j in range(ipb * (tt // LANES)):
            weight_slab(s, j)
        z = act.astype(BF16)
        pt = w_scr[...] * (z * (1.0 + lax.erf(z)))
        d = _dot_tn(pt, v_ref[s * sb:(s + 1) * sb, :])
        part = d if part is None else part + d
        act = act_next
    acc_ref[...] += part

    @pl.when(e == pl.num_programs(1) - 1)
    def _():
        out_ref[...] = _rms(x_ref[...] + acc_ref[...], fnw_ref[...])


def _peer_eval(xn, x2, u16, v16, c0, ap, r1, bp, fnw, *, tt, eb, sb):
    t = x2.shape[0]
    ne = u16.shape[0]
    ib = eb // PEER_KEYS
    kern = functools.partial(_peer_eval_kernel, eb=eb, sb=sb)
    half0 = pl.BlockSpec((PEER_HEADS, ib, tt), lambda i, e: (0, e, i))
    ktiles = PEER_KEYS // BF16_ROWS
    r1 = r1.reshape(PEER_HEADS, ktiles, BF16_ROWS, t)
    bp = bp.reshape(PEER_HEADS, ktiles, BF16_ROWS, t)
    half1 = pl.BlockSpec((PEER_HEADS, ktiles, BF16_ROWS, tt), lambda i, e: (0, 0, 0, i))
    return pl.pallas_call(
        kern,
        out_shape=jax.ShapeDtypeStruct((t, D_MODEL), F32),
        grid=(t // tt, ne // eb),
        in_specs=[pl.BlockSpec((tt, D_MODEL), lambda i, e: (i, 0)),
                  pl.BlockSpec((tt, D_MODEL), lambda i, e: (i, 0)),
                  pl.BlockSpec((eb, D_MODEL), lambda i, e: (e, 0)),
                  pl.BlockSpec((eb, D_MODEL), lambda i, e: (e, 0)),
                  half0, half0, half1, half1,
                  pl.BlockSpec((1, D_MODEL), lambda i, e: (0, 0))],
        out_specs=pl.BlockSpec((tt, D_MODEL), lambda i, e: (i, 0)),
        scratch_shapes=[pltpu.VMEM((tt, D_MODEL), F32), pltpu.VMEM((sb, tt), BF16)],
        compiler_params=pltpu.CompilerParams(dimension_semantics=("parallel", "arbitrary"),
                                             vmem_limit_bytes=VMEM_LIMIT),
        name="peer_eval",
    )(xn, x2, u16, v16, c0, ap, r1, bp, fnw)


def kernel(x, mem, norm_mix_w, w_in, gdn_conv_w, gdn_a_log, gdn_dt_bias, gdn_norm_w, mlstm_conv_w,
           mlstm_i_bias, mlstm_f_bias, mlstm_norm_w, w_out, norm_xa_w, norm_mem_w, xa_wq, xa_wkv,
           xa_wo, norm_ffn_w, peer_wq, peer_sub_keys, peer_u, peer_v, norm_final_w):
    batch, seq, d = x.shape
    t = batch * seq
    assert d == D_MODEL and w_in.shape[0] == 1, "single-layer block; the final norm is fused into it"
    tm = min(256, seq)
    ts = min(256, seq)
    tt = min(512, t)
    xs = x.reshape(t, d)
    mem2d = mem.reshape(batch * MEM_TOKENS, d)

    wl = w_in[0]
    o_gz = 3 * MIX_W
    o_ga = o_gz + MIX_W
    o_gb = o_ga + MIX_HEADS
    o_mqk = o_gb + MIX_HEADS
    o_mv = o_mqk + 2 * MIX_W
    o_mo = o_mv + MIX_W
    o_mi = o_mo + MIX_W
    o_mf = o_mi + MIX_HEADS
    w_all = jnp.concatenate([wl[:, 0:o_gz], wl[:, o_mqk:o_mv], wl[:, o_gz:o_ga],
                             wl[:, o_mv:o_mo], wl[:, o_mo:o_mi],
                             wl[:, o_ga:o_mqk], wl[:, o_mi:o_mf + MIX_HEADS],
                             jnp.zeros((d, GATE_COLS - 4 * MIX_HEADS), wl.dtype)], axis=1).astype(BF16)
    cw = jnp.concatenate([gdn_conv_w[0], mlstm_conv_w[0]], axis=1).astype(F32)
    zeros4 = jnp.zeros((MIX_HEADS,), F32)
    pad = jnp.zeros((GATE_COLS - 4 * MIX_HEADS,), F32)
    gbias = jnp.concatenate([gdn_dt_bias[0], zeros4, mlstm_i_bias[0], mlstm_f_bias[0], pad]
                            ).astype(F32).reshape(1, GATE_COLS)
    alog = jnp.concatenate([gdn_a_log[0], zeros4, zeros4, zeros4, pad]).astype(F32).reshape(1, GATE_COLS)

    kmem, vmem = _memkv(mem2d, norm_mem_w[0].reshape(1, d), xa_wkv[0].astype(BF16))
    qkv, zvo, gates = _inproj(xs, norm_mix_w[0].reshape(1, d), w_all, cw, gbias, alog,
                              seq=seq, tm=tm)
    go = _gdn(qkv, gates, batch=batch, seq=seq, ts=ts)
    mh = _mlstm(qkv, zvo, gates, batch=batch, seq=seq, ts=ts)
    x2 = _mix_xattn(xs, go, mh, zvo, gdn_norm_w[0].reshape(1, HEAD_DIM),
                    mlstm_norm_w[0].reshape(1, MIX_W), w_out[0].astype(BF16),
                    norm_xa_w[0].reshape(1, d), xa_wq[0].astype(BF16),
                    kmem.reshape(batch, MEM_TOKENS, d), vmem.reshape(batch, MEM_TOKENS, d),
                    xa_wo[0].astype(BF16), seq=seq, tm=tm)
    xn, c0, ap, r1, bp = _peer_route(x2, norm_ffn_w[0].reshape(1, d), peer_wq[0].astype(BF16),
                                     peer_sub_keys[0].astype(BF16), tt=tt)
    u16 = (peer_u[0] * math.sqrt(0.5)).astype(BF16)
    out = _peer_eval(xn, x2, u16, peer_v[0].astype(BF16), c0, ap, r1, bp,
                     norm_final_w.reshape(1, d), tt=tt, eb=4096, sb=256)
    return out.reshape(batch, seq, d)
```

```python
import functools
import math

import jax
import jax.numpy as jnp
from jax import lax
from jax.experimental import pallas as pl
from jax.experimental.pallas import tpu as pltpu

F32 = jnp.float32
BF16 = jnp.bfloat16
HIGHEST = lax.Precision.HIGHEST

D_MODEL = 1024
HEAD_DIM = 128
MIX_HEADS = 4
MIX_W = MIX_HEADS * HEAD_DIM
CONV_WIDTH = 4
CHUNK = 64
MEM_TOKENS = 256
XA_HEADS = 4
XA_HEAD_DIM = D_MODEL // XA_HEADS
PEER_HEADS = 8
PEER_KEYS = 128
PEER_TOPK = 16
PEER_HALF = 128
NORM_EPS = 1e-6
NEG_BIG = -1e30
LANES = 128
MXU_COLS = 256
BF16_ROWS = 16
RANK_SCALE = 2.0

CONV_COLS = 3 * MIX_W + 2 * MIX_W
PLAIN_COLS = 3 * MIX_W
GATE_COLS = 128
PROJ_COLS = CONV_COLS + PLAIN_COLS + GATE_COLS
HALO = 8

VMEM_LIMIT = 60 * 1024 * 1024

NT_DIMS = (((1,), (1,)), ((), ()))
TN_DIMS = (((0,), (0,)), ((), ()))


def _rms(x, w):
    return x * lax.rsqrt(jnp.mean(x * x, axis=-1, keepdims=True) + NORM_EPS) * w


def _sigmoid(x):
    return 1.0 / (1.0 + jnp.exp(-x))


def _softplus(x):
    return jnp.maximum(x, 0.0) + jnp.log1p(jnp.exp(-jnp.abs(x)))


def _dot(a, b):
    return jnp.dot(a, b, preferred_element_type=F32)


def _dot_nt(a, b, precision=None):
    return lax.dot_general(a, b, NT_DIMS, precision=precision, preferred_element_type=F32)


def _dot_tn(a, b):
    return lax.dot_general(a, b, TN_DIMS, preferred_element_type=F32)


def _memkv_kernel(m_ref, nw_ref, w_ref, k_ref, v_ref):
    mn = _rms(m_ref[...], nw_ref[...]).astype(BF16)
    kv = _dot(mn, w_ref[...])
    k_ref[...] = kv[:, :D_MODEL].astype(BF16)
    v_ref[...] = kv[:, D_MODEL:].astype(BF16)


def _memkv(mem2d, nw, wkv):
    n = mem2d.shape[0]
    tm = MEM_TOKENS
    return pl.pallas_call(
        _memkv_kernel,
        out_shape=(jax.ShapeDtypeStruct((n, D_MODEL), BF16),
                   jax.ShapeDtypeStruct((n, D_MODEL), BF16)),
        grid=(n // tm,),
        in_specs=[pl.BlockSpec((tm, D_MODEL), lambda i: (i, 0)),
                  pl.BlockSpec((1, D_MODEL), lambda i: (0, 0)),
                  pl.BlockSpec((D_MODEL, 2 * D_MODEL), lambda i: (0, 0))],
        out_specs=(pl.BlockSpec((tm, D_MODEL), lambda i: (i, 0)),
                   pl.BlockSpec((tm, D_MODEL), lambda i: (i, 0))),
        compiler_params=pltpu.CompilerParams(dimension_semantics=("parallel",),
                                             vmem_limit_bytes=VMEM_LIMIT),
        name="memkv",
    )(mem2d, nw, wkv)


def _inproj_kernel(x_ref, halo_ref, nw_ref, w_ref, cw_ref, gbias_ref, alog_ref,
                   qkv_ref, zvo_ref, gate_ref, p_scr, *, tm, seq):
    i = pl.program_id(0)
    xc = jnp.concatenate([halo_ref[...], x_ref[...]], axis=0)
    xn = _rms(xc, nw_ref[...]).astype(BF16)
    p = _dot(xn, w_ref[...])

    first_row = jnp.where((i * tm) % seq == 0, HALO, 0)
    row = lax.broadcasted_iota(jnp.int32, (HALO, 1), 0)
    p_scr[:HALO, :] = jnp.where(row >= first_row, p[:HALO, :CONV_COLS], 0.0)
    p_scr[HALO:, :] = p[HALO:, :CONV_COLS]
    cw = cw_ref[...]
    y = cw[CONV_WIDTH - 1:CONV_WIDTH] * p_scr[HALO:HALO + tm, :]
    for kk in range(CONV_WIDTH - 1):
        off = HALO - (CONV_WIDTH - 1) + kk
        y = y + cw[kk:kk + 1] * p_scr[off:off + tm, :]
    y = y * _sigmoid(y)

    scale = HEAD_DIM ** -0.5
    for hh in range(2 * MIX_HEADS):
        cols = slice(hh * HEAD_DIM, (hh + 1) * HEAD_DIM)
        yh = y[:, cols]
        inv = lax.rsqrt(jnp.sum(yh * yh, axis=-1, keepdims=True) + NORM_EPS)
        if hh < MIX_HEADS:
            inv = inv * scale
        qkv_ref[:, cols] = yh * inv
    qkv_ref[:, 2 * MIX_W:4 * MIX_W] = y[:, 2 * MIX_W:4 * MIX_W]
    qkv_ref[:, 4 * MIX_W:5 * MIX_W] = y[:, 4 * MIX_W:5 * MIX_W] * scale

    zvo_ref[...] = p[HALO:, CONV_COLS:CONV_COLS + PLAIN_COLS]

    ga = p[HALO:, CONV_COLS + PLAIN_COLS:] + gbias_ref[...]
    lane = lax.broadcasted_iota(jnp.int32, (tm, GATE_COLS), 1)
    log_alpha = -jnp.exp(alog_ref[...]) * _softplus(ga)
    beta = _sigmoid(ga)
    log_f = -_softplus(-ga)
    gate_ref[...] = jnp.where(lane < 4, log_alpha,
                              jnp.where(lane < 8, beta,
                                        jnp.where(lane < 12, ga,
                                                  jnp.where(lane < 16, log_f, 0.0))))


def _inproj(x2d, nw, w_all, cw, gbias, alog, *, seq, tm):
    t = x2d.shape[0]
    hb = tm // HALO
    kern = functools.partial(_inproj_kernel, tm=tm, seq=seq)
    return pl.pallas_call(
        kern,
        out_shape=(jax.ShapeDtypeStruct((t, CONV_COLS), F32),
                   jax.ShapeDtypeStruct((t, PLAIN_COLS), F32),
                   jax.ShapeDtypeStruct((t, GATE_COLS), F32)),
        grid=(t // tm,),
        in_specs=[pl.BlockSpec((tm, D_MODEL), lambda i: (i, 0)),
                  pl.BlockSpec((HALO, D_MODEL), lambda i: (jnp.maximum(i * hb - 1, 0), 0)),
                  pl.BlockSpec((1, D_MODEL), lambda i: (0, 0)),
                  pl.BlockSpec((D_MODEL, PROJ_COLS), lambda i: (0, 0)),
                  pl.BlockSpec((CONV_WIDTH, CONV_COLS), lambda i: (0, 0)),
                  pl.BlockSpec((1, GATE_COLS), lambda i: (0, 0)),
                  pl.BlockSpec((1, GATE_COLS), lambda i: (0, 0))],
        out_specs=(pl.BlockSpec((tm, CONV_COLS), lambda i: (i, 0)),
                   pl.BlockSpec((tm, PLAIN_COLS), lambda i: (i, 0)),
                   pl.BlockSpec((tm, GATE_COLS), lambda i: (i, 0))),
        scratch_shapes=[pltpu.VMEM((HALO + tm, CONV_COLS), F32)],
        compiler_params=pltpu.CompilerParams(dimension_semantics=("parallel",),
                                             vmem_limit_bytes=VMEM_LIMIT),
        name="inproj",
    )(x2d, x2d, nw, w_all, cw, gbias, alog)


def _chunk_cumsum(g, ts):
    ri = lax.broadcasted_iota(jnp.int32, (ts, ts), 0)
    ci = lax.broadcasted_iota(jnp.int32, (ts, ts), 1)
    shift = CHUNK.bit_length() - 1
    same_chunk = jnp.right_shift(ri, shift) == jnp.right_shift(ci, shift)
    tri = jnp.where(jnp.logical_and(same_chunk, ci <= ri), 1.0, 0.0)
    return jnp.dot(tri, g, precision=HIGHEST, preferred_element_type=F32)


def _lane_pick(x, lane, idx):
    return jnp.where(lane == idx, x, 0.0)


def _gdn_kernel(q_ref, k_ref, v_ref, g_ref, o_ref, s_ref, *, ts):
    @pl.when(pl.program_id(1) == 0)
    def _():
        s_ref[...] = jnp.zeros_like(s_ref)

    L = CHUNK
    nch = ts // L
    g = g_ref[...]
    cs = _chunk_cumsum(g, ts)
    lane = lax.broadcasted_iota(jnp.int32, (L, GATE_COLS), 1)
    r = lax.broadcasted_iota(jnp.int32, (L, L), 0)
    c = lax.broadcasted_iota(jnp.int32, (L, L), 1)
    causal = r >= c
    strict = r > c
    eye = jnp.where(r == c, 1.0, 0.0)

    items = [(ch, h) for ch in range(nch) for h in range(MIX_HEADS)]
    n = len(items)

    def rows_of(ch):
        return slice(ch * L, (ch + 1) * L)

    def cols_of(h):
        return slice(h * HEAD_DIM, (h + 1) * HEAD_DIM)

    q = [q_ref[rows_of(ch), cols_of(h)] for ch, h in items]
    k = [k_ref[rows_of(ch), cols_of(h)] for ch, h in items]
    v = [v_ref[rows_of(ch), cols_of(h)] for ch, h in items]
    gc = [cs[rows_of(ch), h:h + 1] for ch, h in items]
    beta = [g[rows_of(ch), 4 + h:5 + h] for ch, h in items]
    gl = [cs[(ch + 1) * L - 1:(ch + 1) * L, h:h + 1] for ch, h in items]

    diff = []
    for ch, h in items:
        e_h = jnp.where(lane == h, 1.0, 0.0)
        cs_h = _lane_pick(cs[rows_of(ch)], lane, h)
        diff.append(_dot_nt(jnp.concatenate([cs_h, e_h], axis=1),
                            jnp.concatenate([e_h, -cs_h], axis=1), precision=HIGHEST))
    decay = [jnp.exp(jnp.where(causal, d, NEG_BIG)) for d in diff]
    eg = [jnp.exp(x) for x in gc]
    kb = [k[i] * beta[i] for i in range(n)]
    k16 = [x.astype(BF16) for x in k]
    q16 = [x.astype(BF16) for x in q]
    kk = [_dot_nt(kb[i].astype(BF16), k16[i]) for i in range(n)]
    qk = [_dot_nt(q16[i], k16[i]) for i in range(n)]
    a = [jnp.where(strict, kk[i] * decay[i], 0.0) for i in range(n)]
    qk16 = [(qk[i] * decay[i]).astype(BF16) for i in range(n)]
    tinv = [eye - x for x in a]
    pw = a
    for _ in range(5):
        pw16 = [x.astype(BF16) for x in pw]
        pw = [_dot(x, x) for x in pw16]
        tinv = [tinv[i] + _dot(tinv[i].astype(BF16), pw[i].astype(BF16)) for i in range(n)]
    rhs = [jnp.concatenate([v[i] * beta[i], kb[i] * eg[i]], axis=1).astype(BF16) for i in range(n)]
    uw = [_dot(tinv[i].astype(BF16), rhs[i]) for i in range(n)]
    wq16 = [jnp.concatenate([uw[i][:, HEAD_DIM:], q[i] * eg[i]], axis=0).astype(BF16)
            for i in range(n)]
    kd16 = [(k[i] * jnp.exp(gl[i] - gc[i])).astype(BF16) for i in range(n)]
    egl = [jnp.exp(x) for x in gl]

    state = [s_ref[h] for h in range(MIX_HEADS)]
    for ch in range(nch):
        base = ch * MIX_HEADS
        rr = [_dot(wq16[base + h], state[h].astype(BF16)) for h in range(MIX_HEADS)]
        vnew16 = [(uw[base + h][:, :HEAD_DIM] - rr[h][:L]).astype(BF16) for h in range(MIX_HEADS)]
        for h in range(MIX_HEADS):
            o_ref[rows_of(ch), cols_of(h)] = rr[h][L:] + _dot(qk16[base + h], vnew16[h])
        state = [state[h] * egl[base + h] + _dot_tn(kd16[base + h], vnew16[h])
                 for h in range(MIX_HEADS)]
    for h in range(MIX_HEADS):
        s_ref[h] = state[h]


def _gdn(qkv, gates, *, batch, seq, ts):
    t = qkv.shape[0]
    nt = seq // ts
    kern = functools.partial(_gdn_kernel, ts=ts)

    def col(j):
        return pl.BlockSpec((ts, MIX_W), lambda b, s, j=j: (b * nt + s, j))

    return pl.pallas_call(
        kern,
        out_shape=jax.ShapeDtypeStruct((t, MIX_W), F32),
        grid=(batch, nt),
        in_specs=[col(0), col(1), col(2),
                  pl.BlockSpec((ts, GATE_COLS), lambda b, s: (b * nt + s, 0))],
        out_specs=pl.BlockSpec((ts, MIX_W), lambda b, s: (b * nt + s, 0)),
        scratch_shapes=[pltpu.VMEM((MIX_HEADS, HEAD_DIM, HEAD_DIM), F32)],
        compiler_params=pltpu.CompilerParams(dimension_semantics=("parallel", "arbitrary"),
                                             vmem_limit_bytes=VMEM_LIMIT),
        name="gdn",
    )(qkv, qkv, qkv, gates)


def _mlstm_kernel(q_ref, k_ref, v_ref, g_ref, o_ref, c_ref, m_ref, *, ts):
    @pl.when(pl.program_id(1) == 0)
    def _():
        c_ref[...] = jnp.zeros_like(c_ref)
        m_ref[...] = jnp.zeros_like(m_ref)

    L = CHUNK
    nch = ts // L
    g = g_ref[...]
    cs = _chunk_cumsum(g, ts)
    lane = lax.broadcasted_iota(jnp.int32, (L, GATE_COLS), 1)
    r = lax.broadcasted_iota(jnp.int32, (L, L), 0)
    c = lax.broadcasted_iota(jnp.int32, (L, L), 1)
    causal = r >= c
    ones = jnp.ones((L, HEAD_DIM), F32)

    items = [(ch, h) for ch in range(nch) for h in range(MIX_HEADS)]
    n = len(items)

    def rows_of(ch):
        return slice(ch * L, (ch + 1) * L)

    def cols_of(h):
        return slice(h * HEAD_DIM, (h + 1) * HEAD_DIM)

    q16 = [q_ref[rows_of(ch), cols_of(h)].astype(BF16) for ch, h in items]
    k = [k_ref[rows_of(ch), cols_of(h)] for ch, h in items]
    vaug = [jnp.concatenate([v_ref[rows_of(ch), cols_of(h)], ones], axis=1).astype(BF16)
            for ch, h in items]
    bc = [cs[rows_of(ch), 12 + h:13 + h] for ch, h in items]
    li = [g[rows_of(ch), 8 + h:9 + h] for ch, h in items]
    bl = [cs[(ch + 1) * L - 1:(ch + 1) * L, 12 + h:13 + h] for ch, h in items]
    dmat = []
    for ch, h in items:
        csch = cs[rows_of(ch)]
        gch = g[rows_of(ch)]
        e_f = jnp.where(lane == 12 + h, 1.0, 0.0)
        e_if = jnp.where(jnp.logical_or(lane == 8 + h, lane == 12 + h), 1.0, 0.0)
        xm = jnp.concatenate([_lane_pick(csch, lane, 12 + h), e_if], axis=1)
        ym = jnp.concatenate([e_f, _lane_pick(gch, lane, 8 + h) - _lane_pick(csch, lane, 12 + h)],
                             axis=1)
        dmat.append(jnp.where(causal, _dot_nt(xm, ym, precision=HIGHEST), NEG_BIG))
    qk = [_dot_nt(q16[i], k[i].astype(BF16)) for i in range(n)]
    rowmax = [jnp.max(d, axis=1, keepdims=True) for d in dmat]
    dl = [bl[i] - bc[i] + li[i] for i in range(n)]
    dlmax = [jnp.max(x, axis=0, keepdims=True) for x in dl]

    m_in = []
    m_cur = [m_ref[h][0:1, 0:1] for h in range(MIX_HEADS)]
    for i, (ch, h) in enumerate(items):
        m_in.append(m_cur[h])
        m_cur[h] = jnp.maximum(bl[i] + m_cur[h], dlmax[i])
    m_out = [m_cur[h] if ch == nch - 1 else m_in[(ch + 1) * MIX_HEADS + h] for ch, h in items]
    for h in range(MIX_HEADS):
        m_ref[h] = jnp.broadcast_to(m_cur[h], m_ref.shape[1:])

    inter = [bc[i] + m_in[i] for i in range(n)]
    mt = [jnp.maximum(inter[i], rowmax[i]) for i in range(n)]
    p16 = [(qk[i] * jnp.exp(dmat[i] - mt[i])).astype(BF16) for i in range(n)]
    sc = [jnp.exp(inter[i] - mt[i]) for i in range(n)]
    pv = [_dot(p16[i], vaug[i]) for i in range(n)]
    kw16 = [(k[i] * jnp.exp(dl[i] - m_out[i])).astype(BF16) for i in range(n)]
    kv = [_dot_tn(kw16[i], vaug[i]) for i in range(n)]
    dec = [jnp.exp(bl[i] + m_in[i] - m_out[i]) for i in range(n)]

    cst = [c_ref[h] for h in range(MIX_HEADS)]
    c_in = []
    for i, (ch, h) in enumerate(items):
        c_in.append(cst[h].astype(BF16))
        cst[h] = dec[i] * cst[h] + kv[i]
    for h in range(MIX_HEADS):
        c_ref[h] = cst[h]

    for i, (ch, h) in enumerate(items):
        nd = sc[i] * _dot(q16[i], c_in[i]) + pv[i]
        den = jnp.maximum(jnp.abs(nd[:, HEAD_DIM:]), jnp.exp(-mt[i]))
        o_ref[rows_of(ch), cols_of(h)] = nd[:, :HEAD_DIM] / den


def _mlstm(qkv, zvo, gates, *, batch, seq, ts):
    t = qkv.shape[0]
    nt = seq // ts
    kern = functools.partial(_mlstm_kernel, ts=ts)

    def col(j):
        return pl.BlockSpec((ts, MIX_W), lambda b, s, j=j: (b * nt + s, j))

    return pl.pallas_call(
        kern,
        out_shape=jax.ShapeDtypeStruct((t, MIX_W), F32),
        grid=(batch, nt),
        in_specs=[col(3), col(4), col(1),
                  pl.BlockSpec((ts, GATE_COLS), lambda b, s: (b * nt + s, 0))],
        out_specs=pl.BlockSpec((ts, MIX_W), lambda b, s: (b * nt + s, 0)),
        scratch_shapes=[pltpu.VMEM((MIX_HEADS, HEAD_DIM, 2 * HEAD_DIM), F32),
                        pltpu.VMEM((MIX_HEADS, 8, 128), F32)],
        compiler_params=pltpu.CompilerParams(dimension_semantics=("parallel", "arbitrary"),
                                             vmem_limit_bytes=VMEM_LIMIT),
        name="mlstm",
    )(qkv, qkv, zvo, gates)


def _mix_xattn_kernel(x_ref, go_ref, mh_ref, z_ref, mo_ref, gnw_ref, mnw_ref, wout_ref,
                      nxa_ref, wq_ref, k_ref, v_ref, wo_ref, out_ref):
    parts = []
    gnw = gnw_ref[...]
    for h in range(MIX_HEADS):
        cols = slice(h * HEAD_DIM, (h + 1) * HEAD_DIM)
        z = z_ref[:, cols]
        parts.append(_rms(go_ref[:, cols], gnw) * (z * _sigmoid(z)))
    for h in range(MIX_HEADS):
        cols = slice(h * HEAD_DIM, (h + 1) * HEAD_DIM)
        parts.append(_rms(mh_ref[:, cols], mnw_ref[:, cols]) * _sigmoid(mo_ref[:, cols]))
    mixed = jnp.concatenate(parts, axis=1).astype(BF16)
    x1 = x_ref[...] + _dot(mixed, wout_ref[...])

    xn = _rms(x1, nxa_ref[...]).astype(BF16)
    q = _dot(xn, wq_ref[...])
    kk = k_ref[0]
    vv = v_ref[0]
    outs = []
    for h in range(XA_HEADS):
        cols = slice(h * XA_HEAD_DIM, (h + 1) * XA_HEAD_DIM)
        s = _dot_nt(q[:, cols].astype(BF16), kk[:, cols]) * (XA_HEAD_DIM ** -0.5)
        s = s - jnp.max(s, axis=-1, keepdims=True)
        e = jnp.exp(s)
        p = e / jnp.sum(e, axis=-1, keepdims=True)
        outs.append(_dot(p.astype(BF16), vv[:, cols]))
    o = jnp.concatenate(outs, axis=1).astype(BF16)
    out_ref[...] = x1 + _dot(o, wo_ref[...])


def _mix_xattn(x2d, go, mh, zvo, gnw, mnw, wout, nxa, wq, kmem, vmem, wo, *, seq, tm):
    t = x2d.shape[0]
    per_seq = seq // tm

    def row(width, j=0):
        return pl.BlockSpec((tm, width), lambda i, j=j: (i, j))

    def const(shape):
        return pl.BlockSpec(shape, lambda i: (0,) * len(shape))

    mem_spec = pl.BlockSpec((1, MEM_TOKENS, D_MODEL), lambda i: (i // per_seq, 0, 0))
    return pl.pallas_call(
        _mix_xattn_kernel,
        out_shape=jax.ShapeDtypeStruct((t, D_MODEL), F32),
        grid=(t // tm,),
        in_specs=[row(D_MODEL), row(MIX_W), row(MIX_W), row(MIX_W, 0), row(MIX_W, 2),
                  const((1, HEAD_DIM)), const((1, MIX_W)), const((D_MODEL, D_MODEL)),
                  const((1, D_MODEL)), const((D_MODEL, D_MODEL)), mem_spec, mem_spec,
                  const((D_MODEL, D_MODEL))],
        out_specs=row(D_MODEL),
        compiler_params=pltpu.CompilerParams(dimension_semantics=("parallel",),
                                             vmem_limit_bytes=VMEM_LIMIT),
        name="mix_xattn",
    )(x2d, go, mh, zvo, zvo, gnw, mnw, wout, nxa, wq, kmem, vmem, wo)


def _take_top16(values, order, exact):
    tt = values.shape[1]
    slot = lax.broadcasted_iota(jnp.int32, (PEER_TOPK, tt), 0)
    v = values
    step = jnp.full(values.shape, float(PEER_TOPK), F32)
    vals = jnp.zeros((PEER_TOPK, tt), F32)
    for kk in range(PEER_TOPK):
        m = jnp.max(v, axis=0, keepdims=True)
        if exact:
            first = jnp.min(jnp.where(v == m, order, 1e9), axis=0, keepdims=True)
            sel = order == first
        else:
            sel = v == m
        step = jnp.where(sel, float(kk), step)
        v = jnp.where(sel, -jnp.inf, v)
        vals = jnp.where(slot == kk, m, vals)
    count = jnp.sum(jnp.where(step < float(PEER_TOPK), 1.0, 0.0), axis=0, keepdims=True)
    return step, vals, count


_CAND_GROUPS = ((0, None, 0), (0, None, 8), (1, None, 0), (2, None, 0),
                (3, 4, 0), (5, 6, 0), (7, 8, 0), (9, 10, 0), (11, 12, 0), (13, 14, 0),
                (15, None, 0))


def _route_kernel(x_ref, nw_ref, wq_ref, keys_ref, xnt_ref, c0_ref, ap_ref, r1_ref, bp_ref,
                  xn_scr, *, tt):
    @pl.when(pl.program_id(1) == 0)
    def _():
        xn32 = _rms(x_ref[...], nw_ref[...])
        xn_scr[...] = xn32.astype(BF16)
        xnt_ref[...] = xn32.T.astype(BF16)

    xn = xn_scr[...]
    q = _dot(xn, wq_ref[...])
    s0 = _dot_nt(keys_ref[0, 0], q[:, :PEER_HALF].astype(BF16))
    s1 = _dot_nt(keys_ref[0, 1], q[:, PEER_HALF:].astype(BF16))

    def tables(exact):
        kidx = lax.broadcasted_iota(jnp.int32, s0.shape, 0).astype(F32)
        r0, a, n0 = _take_top16(s0, kidx, exact)
        r1, b, n1 = _take_top16(s1, kidx, exact)

        sub = lax.broadcasted_iota(jnp.int32, (8, 1), 0)
        b_lo4 = jnp.where(sub < 4, b[0:8], pltpu.roll(b[0:8], 4, 0))
        cands, poss, valids, prow = [], [], [], []
        for (p0, p1, q0) in _CAND_GROUPS:
            if p1 is None:
                av = jnp.broadcast_to(a[p0:p0 + 1], (8, tt))
                bv = b[q0:q0 + 8]
                pr = jnp.full((8, 1), p0, jnp.int32)
                qr = sub + q0
            else:
                av = jnp.where(sub < 4, a[p0:p0 + 1], a[p1:p1 + 1])
                bv = b_lo4
                pr = jnp.where(sub < 4, p0, p1)
                qr = sub % 4
            cands.append(av + bv)
            poss.append((pr * PEER_TOPK + qr).astype(F32))
            valids.append((pr + 1) * (qr + 1) <= PEER_TOPK)
            prow.append(pr)
        cand = jnp.concatenate(cands, axis=0)
        pos = jnp.broadcast_to(jnp.concatenate(poss, axis=0), cand.shape)
        valid = jnp.concatenate(valids, axis=0)
        pidx = jnp.concatenate(prow, axis=0)
        taken, _, n2 = _take_top16(jnp.where(valid, cand, -jnp.inf), pos, exact)
        chosen = jnp.where(taken < float(PEER_TOPK), 1.0, 0.0)

        top = a[0:1] + b[0:1]
        gsel = chosen * jnp.exp(jnp.where(valid, cand, top) - top)
        zinv = 1.0 / jnp.sum(gsel, axis=0, keepdims=True)

        c0 = jnp.zeros(r0.shape, F32)
        for p in range(PEER_TOPK):
            cnt = jnp.sum(jnp.where(pidx == p, chosen, 0.0), axis=0, keepdims=True)
            c0 = jnp.where(r0 == float(p), cnt, c0)

        c0_ref[0] = c0
        ap_ref[0] = jnp.exp(s0 - a[0:1]) * (zinv * math.sqrt(0.5))
        r1_ref[0] = (r1 * RANK_SCALE).astype(BF16)
        bp_ref[0] = jnp.exp(s1 - b[0:1]).astype(BF16)
        return n0 + n1 + n2

    taken_total = tables(exact=False)
    tied = jnp.max(jnp.abs(taken_total - 3.0 * PEER_TOPK)) > 0.0

    @pl.when(tied)
    def _():
        tables(exact=True)


def _peer_route(x2, nw, wq, keys, *, tt):
    t = x2.shape[0]
    kern = functools.partial(_route_kernel, tt=tt)
    tab = lambda dt: jax.ShapeDtypeStruct((PEER_HEADS, PEER_KEYS, t), dt)
    tab_spec = pl.BlockSpec((1, PEER_KEYS, tt), lambda i, h: (h, 0, i))
    return pl.pallas_call(
        kern,
        out_shape=(jax.ShapeDtypeStruct((D_MODEL, t), BF16), tab(F32), tab(F32), tab(BF16), tab(BF16)),
        grid=(t // tt, PEER_HEADS),
        in_specs=[pl.BlockSpec((tt, D_MODEL), lambda i, h: (i, 0)),
                  pl.BlockSpec((1, D_MODEL), lambda i, h: (0, 0)),
                  pl.BlockSpec((D_MODEL, 2 * PEER_HALF), lambda i, h: (0, h)),
                  pl.BlockSpec((1, 2, PEER_KEYS, PEER_HALF), lambda i, h: (h, 0, 0, 0))],
        out_specs=(pl.BlockSpec((D_MODEL, tt), lambda i, h: (0, i)),
                   tab_spec, tab_spec, tab_spec, tab_spec),
        scratch_shapes=[pltpu.VMEM((tt, D_MODEL), BF16)],
        compiler_params=pltpu.CompilerParams(dimension_semantics=("parallel", "arbitrary"),
                                             vmem_limit_bytes=VMEM_LIMIT),
        name="peer_route",
    )(x2, nw, wq, keys)


def _peer_eval_kernel(xnt_ref, x_ref, u_ref, v_ref, c0_ref, ap_ref, r1_ref, bp_ref, fnw_ref,
                      out_ref, acc_ref, w_scr, *, eb, sb):
    e = pl.program_id(1)

    @pl.when(e == 0)
    def _():
        acc_ref[...] = jnp.zeros_like(acc_ref)

    tt = xnt_ref.shape[1]
    nsub = eb // sb
    ipb = sb // PEER_KEYS

    def activations(s):
        return _dot(u_ref[s * sb:(s + 1) * sb, :], xnt_ref[...])

    def row_tile(ref, h, il, lanes, scale):
        row = ref[h, il:il + 1, lanes] * scale
        return jnp.broadcast_to(row, (BF16_ROWS, LANES)).astype(BF16)[None]

    def weight_slab(s, j):
        ii, lg = divmod(j, tt // LANES)
        il = s * ipb + ii
        lanes = slice(lg * LANES, (lg + 1) * LANES)
        w = None
        for h in range(PEER_HEADS):
            gate = jnp.clip(row_tile(c0_ref, h, il, lanes, RANK_SCALE) - r1_ref[h, :, :, lanes],
                            0.0, row_tile(ap_ref, h, il, lanes, 1.0))
            term = gate * bp_ref[h, :, :, lanes]
            w = term if w is None else w + term
        w_scr[ii * PEER_KEYS:(ii + 1) * PEER_KEYS, lanes] = w.reshape(PEER_KEYS, LANES)

    act = activations(0)
    part = None
    for s in range(nsub):
        act_next = activations(s + 1) if s + 1 < nsub else None
        for j in range(ipb * (tt // LANES)):
            weight_slab(s, j)
        z = act.astype(BF16)
        pt = w_scr[...] * (z * (1.0 + lax.erf(z)))
        d = _dot_tn(pt, v_ref[s * sb:(s + 1) * sb, :])
        part = d if part is None else part + d
        act = act_next
    acc_ref[...] += part

    @pl.when(e == pl.num_programs(1) - 1)
    def _():
        out_ref[...] = _rms(x_ref[...] + acc_ref[...], fnw_ref[...])


def _peer_eval(xn, x2, u16, v16, c0, ap, r1, bp, fnw, *, tt, eb, sb):
    t = x2.shape[0]
    ne = u16.shape[0]
    ib = eb // PEER_KEYS
    kern = functools.partial(_peer_eval_kernel, eb=eb, sb=sb)
    half0 = pl.BlockSpec((PEER_HEADS, ib, tt), lambda i, e: (0, e, i))
    ktiles = PEER_KEYS // BF16_ROWS
    r1 = r1.reshape(PEER_HEADS, ktiles, BF16_ROWS, t)
    bp = bp.reshape(PEER_HEADS, ktiles, BF16_ROWS, t)
    half1 = pl.BlockSpec((PEER_HEADS, ktiles, BF16_ROWS, tt), lambda i, e: (0, 0, 0, i))
    return pl.pallas_call(
        kern,
        out_shape=jax.ShapeDtypeStruct((t, D_MODEL), F32),
        grid=(t // tt, ne // eb),
        in_specs=[pl.BlockSpec((D_MODEL, tt), lambda i, e: (0, i)),
                  pl.BlockSpec((tt, D_MODEL), lambda i, e: (i, 0)),
                  pl.BlockSpec((eb, D_MODEL), lambda i, e: (e, 0)),
                  pl.BlockSpec((eb, D_MODEL), lambda i, e: (e, 0)),
                  half0, half0, half1, half1,
                  pl.BlockSpec((1, D_MODEL), lambda i, e: (0, 0))],
        out_specs=pl.BlockSpec((tt, D_MODEL), lambda i, e: (i, 0)),
        scratch_shapes=[pltpu.VMEM((tt, D_MODEL), F32), pltpu.VMEM((sb, tt), BF16)],
        compiler_params=pltpu.CompilerParams(dimension_semantics=("parallel", "arbitrary"),
                                             vmem_limit_bytes=VMEM_LIMIT),
        name="peer_eval",
    )(xn, x2, u16, v16, c0, ap, r1, bp, fnw)


def kernel(x, mem, norm_mix_w, w_in, gdn_conv_w, gdn_a_log, gdn_dt_bias, gdn_norm_w, mlstm_conv_w,
           mlstm_i_bias, mlstm_f_bias, mlstm_norm_w, w_out, norm_xa_w, norm_mem_w, xa_wq, xa_wkv,
           xa_wo, norm_ffn_w, peer_wq, peer_sub_keys, peer_u, peer_v, norm_final_w):
    batch, seq, d = x.shape
    t = batch * seq
    assert d == D_MODEL and w_in.shape[0] == 1, "single-layer block; the final norm is fused into it"
    tm = min(256, seq)
    ts = min(256, seq)
    tt = min(512, t)
    xs = x.reshape(t, d)
    mem2d = mem.reshape(batch * MEM_TOKENS, d)

    wl = w_in[0]
    o_gz = 3 * MIX_W
    o_ga = o_gz + MIX_W
    o_gb = o_ga + MIX_HEADS
    o_mqk = o_gb + MIX_HEADS
    o_mv = o_mqk + 2 * MIX_W
    o_mo = o_mv + MIX_W
    o_mi = o_mo + MIX_W
    o_mf = o_mi + MIX_HEADS
    w_all = jnp.concatenate([wl[:, 0:o_gz], wl[:, o_mqk:o_mv], wl[:, o_gz:o_ga],
                             wl[:, o_mv:o_mo], wl[:, o_mo:o_mi],
                             wl[:, o_ga:o_mqk], wl[:, o_mi:o_mf + MIX_HEADS],
                             jnp.zeros((d, GATE_COLS - 4 * MIX_HEADS), wl.dtype)], axis=1).astype(BF16)
    cw = jnp.concatenate([gdn_conv_w[0], mlstm_conv_w[0]], axis=1).astype(F32)
    zeros4 = jnp.zeros((MIX_HEADS,), F32)
    pad = jnp.zeros((GATE_COLS - 4 * MIX_HEADS,), F32)
    gbias = jnp.concatenate([gdn_dt_bias[0], zeros4, mlstm_i_bias[0], mlstm_f_bias[0], pad]
                            ).astype(F32).reshape(1, GATE_COLS)
    alog = jnp.concatenate([gdn_a_log[0], zeros4, zeros4, zeros4, pad]).astype(F32).reshape(1, GATE_COLS)

    kmem, vmem = _memkv(mem2d, norm_mem_w[0].reshape(1, d), xa_wkv[0].astype(BF16))
    qkv, zvo, gates = _inproj(xs, norm_mix_w[0].reshape(1, d), w_all, cw, gbias, alog,
                              seq=seq, tm=tm)
    go = _gdn(qkv, gates, batch=batch, seq=seq, ts=ts)
    mh = _mlstm(qkv, zvo, gates, batch=batch, seq=seq, ts=ts)
    x2 = _mix_xattn(xs, go, mh, zvo, gdn_norm_w[0].reshape(1, HEAD_DIM),
                    mlstm_norm_w[0].reshape(1, MIX_W), w_out[0].astype(BF16),
                    norm_xa_w[0].reshape(1, d), xa_wq[0].astype(BF16),
                    kmem.reshape(batch, MEM_TOKENS, d), vmem.reshape(batch, MEM_TOKENS, d),
                    xa_wo[0].astype(BF16), seq=seq, tm=tm)
    xn, c0, ap, r1, bp = _peer_route(x2, norm_ffn_w[0].reshape(1, d), peer_wq[0].astype(BF16),
                                     peer_sub_keys[0].astype(BF16), tt=tt)
    u16 = (peer_u[0] * math.sqrt(0.5)).astype(BF16)
    out = _peer_eval(xn, x2, u16, peer_v[0].astype(BF16), c0, ap, r1, bp,
                     norm_final_w.reshape(1, d), tt=tt, eb=4096, sb=256)
    return out.reshape(batch, seq, d)
```

```python
import functools
import math

import jax
import jax.numpy as jnp
from jax import lax
from jax.experimental import pallas as pl
from jax.experimental.pallas import tpu as pltpu

F32 = jnp.float32
BF16 = jnp.bfloat16
HIGHEST = lax.Precision.HIGHEST

D_MODEL = 1024
HEAD_DIM = 128
MIX_HEADS = 4
MIX_W = MIX_HEADS * HEAD_DIM
CONV_WIDTH = 4
CHUNK = 64
MEM_TOKENS = 256
XA_HEADS = 4
XA_HEAD_DIM = D_MODEL // XA_HEADS
PEER_HEADS = 8
PEER_KEYS = 128
PEER_TOPK = 16
PEER_HALF = 128
NORM_EPS = 1e-6
NEG_BIG = -1e30
LANES = 128
MXU_COLS = 256
BF16_ROWS = 16
RANK_SCALE = 2.0

CONV_COLS = 3 * MIX_W + 2 * MIX_W
PLAIN_COLS = 3 * MIX_W
GATE_COLS = 128
PROJ_COLS = CONV_COLS + PLAIN_COLS + GATE_COLS
HALO = 8

VMEM_LIMIT = 60 * 1024 * 1024

NT_DIMS = (((1,), (1,)), ((), ()))
TN_DIMS = (((0,), (0,)), ((), ()))


def _rms(x, w):
    return x * lax.rsqrt(jnp.mean(x * x, axis=-1, keepdims=True) + NORM_EPS) * w


def _sigmoid(x):
    return 1.0 / (1.0 + jnp.exp(-x))


def _softplus(x):
    return jnp.maximum(x, 0.0) + jnp.log1p(jnp.exp(-jnp.abs(x)))


def _dot(a, b):
    return jnp.dot(a, b, preferred_element_type=F32)


def _dot_nt(a, b, precision=None):
    return lax.dot_general(a, b, NT_DIMS, precision=precision, preferred_element_type=F32)


def _dot_tn(a, b):
    return lax.dot_general(a, b, TN_DIMS, preferred_element_type=F32)


def _memkv_kernel(m_ref, nw_ref, w_ref, k_ref, v_ref):
    mn = _rms(m_ref[...], nw_ref[...]).astype(BF16)
    kv = _dot(mn, w_ref[...])
    k_ref[...] = kv[:, :D_MODEL].astype(BF16)
    v_ref[...] = kv[:, D_MODEL:].astype(BF16)


def _memkv(mem2d, nw, wkv):
    n = mem2d.shape[0]
    tm = MEM_TOKENS
    return pl.pallas_call(
        _memkv_kernel,
        out_shape=(jax.ShapeDtypeStruct((n, D_MODEL), BF16),
                   jax.ShapeDtypeStruct((n, D_MODEL), BF16)),
        grid=(n // tm,),
        in_specs=[pl.BlockSpec((tm, D_MODEL), lambda i: (i, 0)),
                  pl.BlockSpec((1, D_MODEL), lambda i: (0, 0)),
                  pl.BlockSpec((D_MODEL, 2 * D_MODEL), lambda i: (0, 0))],
        out_specs=(pl.BlockSpec((tm, D_MODEL), lambda i: (i, 0)),
                   pl.BlockSpec((tm, D_MODEL), lambda i: (i, 0))),
        compiler_params=pltpu.CompilerParams(dimension_semantics=("parallel",),
                                             vmem_limit_bytes=VMEM_LIMIT),
        name="memkv",
    )(mem2d, nw, wkv)


def _inproj_kernel(x_ref, halo_ref, nw_ref, w_ref, cw_ref, gbias_ref, alog_ref,
                   qkv_ref, zvo_ref, gate_ref, p_scr, *, tm, seq):
    i = pl.program_id(0)
    xc = jnp.concatenate([halo_ref[...], x_ref[...]], axis=0)
    xn = _rms(xc, nw_ref[...]).astype(BF16)
    p = _dot(xn, w_ref[...])

    first_row = jnp.where((i * tm) % seq == 0, HALO, 0)
    row = lax.broadcasted_iota(jnp.int32, (HALO, 1), 0)
    p_scr[:HALO, :] = jnp.where(row >= first_row, p[:HALO, :CONV_COLS], 0.0)
    p_scr[HALO:, :] = p[HALO:, :CONV_COLS]
    cw = cw_ref[...]
    y = cw[CONV_WIDTH - 1:CONV_WIDTH] * p_scr[HALO:HALO + tm, :]
    for kk in range(CONV_WIDTH - 1):
        off = HALO - (CONV_WIDTH - 1) + kk
        y = y + cw[kk:kk + 1] * p_scr[off:off + tm, :]
    y = y * _sigmoid(y)

    scale = HEAD_DIM ** -0.5
    for hh in range(2 * MIX_HEADS):
        cols = slice(hh * HEAD_DIM, (hh + 1) * HEAD_DIM)
        yh = y[:, cols]
        inv = lax.rsqrt(jnp.sum(yh * yh, axis=-1, keepdims=True) + NORM_EPS)
        if hh < MIX_HEADS:
            inv = inv * scale
        qkv_ref[:, cols] = yh * inv
    qkv_ref[:, 2 * MIX_W:4 * MIX_W] = y[:, 2 * MIX_W:4 * MIX_W]
    qkv_ref[:, 4 * MIX_W:5 * MIX_W] = y[:, 4 * MIX_W:5 * MIX_W] * scale

    zvo_ref[...] = p[HALO:, CONV_COLS:CONV_COLS + PLAIN_COLS]

    ga = p[HALO:, CONV_COLS + PLAIN_COLS:] + gbias_ref[...]
    lane = lax.broadcasted_iota(jnp.int32, (tm, GATE_COLS), 1)
    log_alpha = -jnp.exp(alog_ref[...]) * _softplus(ga)
    beta = _sigmoid(ga)
    log_f = -_softplus(-ga)
    gate_ref[...] = jnp.where(lane < 4, log_alpha,
                              jnp.where(lane < 8, beta,
                                        jnp.where(lane < 12, ga,
                                                  jnp.where(lane < 16, log_f, 0.0))))


def _inproj(x2d, nw, w_all, cw, gbias, alog, *, seq, tm):
    t = x2d.shape[0]
    hb = tm // HALO
    kern = functools.partial(_inproj_kernel, tm=tm, seq=seq)
    return pl.pallas_call(
        kern,
        out_shape=(jax.ShapeDtypeStruct((t, CONV_COLS), F32),
                   jax.ShapeDtypeStruct((t, PLAIN_COLS), F32),
                   jax.ShapeDtypeStruct((t, GATE_COLS), F32)),
        grid=(t // tm,),
        in_specs=[pl.BlockSpec((tm, D_MODEL), lambda i: (i, 0)),
                  pl.BlockSpec((HALO, D_MODEL), lambda i: (jnp.maximum(i * hb - 1, 0), 0)),
                  pl.BlockSpec((1, D_MODEL), lambda i: (0, 0)),
                  pl.BlockSpec((D_MODEL, PROJ_COLS), lambda i: (0, 0)),
                  pl.BlockSpec((CONV_WIDTH, CONV_COLS), lambda i: (0, 0)),
                  pl.BlockSpec((1, GATE_COLS), lambda i: (0, 0)),
                  pl.BlockSpec((1, GATE_COLS), lambda i: (0, 0))],
        out_specs=(pl.BlockSpec((tm, CONV_COLS), lambda i: (i, 0)),
                   pl.BlockSpec((tm, PLAIN_COLS), lambda i: (i, 0)),
                   pl.BlockSpec((tm, GATE_COLS), lambda i: (i, 0))),
        scratch_shapes=[pltpu.VMEM((HALO + tm, CONV_COLS), F32)],
        compiler_params=pltpu.CompilerParams(dimension_semantics=("parallel",),
                                             vmem_limit_bytes=VMEM_LIMIT),
        name="inproj",
    )(x2d, x2d, nw, w_all, cw, gbias, alog)


def _chunk_cumsum(g, ts):
    ri = lax.broadcasted_iota(jnp.int32, (ts, ts), 0)
    ci = lax.broadcasted_iota(jnp.int32, (ts, ts), 1)
    shift = CHUNK.bit_length() - 1
    same_chunk = jnp.right_shift(ri, shift) == jnp.right_shift(ci, shift)
    tri = jnp.where(jnp.logical_and(same_chunk, ci <= ri), 1.0, 0.0)
    return jnp.dot(tri, g, precision=HIGHEST, preferred_element_type=F32)


def _lane_pick(x, lane, idx):
    return jnp.where(lane == idx, x, 0.0)


def _gdn_kernel(q_ref, k_ref, v_ref, g_ref, o_ref, s_ref, *, ts):
    @pl.when(pl.program_id(1) == 0)
    def _():
        s_ref[...] = jnp.zeros_like(s_ref)

    L = CHUNK
    nch = ts // L
    g = g_ref[...]
    cs = _chunk_cumsum(g, ts)
    lane = lax.broadcasted_iota(jnp.int32, (L, GATE_COLS), 1)
    r = lax.broadcasted_iota(jnp.int32, (L, L), 0)
    c = lax.broadcasted_iota(jnp.int32, (L, L), 1)
    causal = r >= c
    strict = r > c
    eye = jnp.where(r == c, 1.0, 0.0)

    items = [(ch, h) for ch in range(nch) for h in range(MIX_HEADS)]
    n = len(items)

    def rows_of(ch):
        return slice(ch * L, (ch + 1) * L)

    def cols_of(h):
        return slice(h * HEAD_DIM, (h + 1) * HEAD_DIM)

    q = [q_ref[rows_of(ch), cols_of(h)] for ch, h in items]
    k = [k_ref[rows_of(ch), cols_of(h)] for ch, h in items]
    v = [v_ref[rows_of(ch), cols_of(h)] for ch, h in items]
    gc = [cs[rows_of(ch), h:h + 1] for ch, h in items]
    beta = [g[rows_of(ch), 4 + h:5 + h] for ch, h in items]
    gl = [cs[(ch + 1) * L - 1:(ch + 1) * L, h:h + 1] for ch, h in items]

    diff = []
    for ch, h in items:
        e_h = jnp.where(lane == h, 1.0, 0.0)
        cs_h = _lane_pick(cs[rows_of(ch)], lane, h)
        diff.append(_dot_nt(jnp.concatenate([cs_h, e_h], axis=1),
                            jnp.concatenate([e_h, -cs_h], axis=1), precision=HIGHEST))
    decay = [jnp.exp(jnp.where(causal, d, NEG_BIG)) for d in diff]
    eg = [jnp.exp(x) for x in gc]
    kb = [k[i] * beta[i] for i in range(n)]
    k16 = [x.astype(BF16) for x in k]
    q16 = [x.astype(BF16) for x in q]
    kk = [_dot_nt(kb[i].astype(BF16), k16[i]) for i in range(n)]
    qk = [_dot_nt(q16[i], k16[i]) for i in range(n)]
    a = [jnp.where(strict, kk[i] * decay[i], 0.0) for i in range(n)]
    qk16 = [(qk[i] * decay[i]).astype(BF16) for i in range(n)]
    tinv = [eye - x for x in a]
    pw = a
    for _ in range(5):
        pw16 = [x.astype(BF16) for x in pw]
        pw = [_dot(x, x) for x in pw16]
        tinv = [tinv[i] + _dot(tinv[i].astype(BF16), pw[i].astype(BF16)) for i in range(n)]
    rhs = [jnp.concatenate([v[i] * beta[i], kb[i] * eg[i]], axis=1).astype(BF16) for i in range(n)]
    uw = [_dot(tinv[i].astype(BF16), rhs[i]) for i in range(n)]
    wq16 = [jnp.concatenate([uw[i][:, HEAD_DIM:], q[i] * eg[i]], axis=0).astype(BF16)
            for i in range(n)]
    kd16 = [(k[i] * jnp.exp(gl[i] - gc[i])).astype(BF16) for i in range(n)]
    egl = [jnp.exp(x) for x in gl]

    state = [s_ref[h] for h in range(MIX_HEADS)]
    for ch in range(nch):
        base = ch * MIX_HEADS
        rr = [_dot(wq16[base + h], state[h].astype(BF16)) for h in range(MIX_HEADS)]
        vnew16 = [(uw[base + h][:, :HEAD_DIM] - rr[h][:L]).astype(BF16) for h in range(MIX_HEADS)]
        for h in range(MIX_HEADS):
            o_ref[rows_of(ch), cols_of(h)] = rr[h][L:] + _dot(qk16[base + h], vnew16[h])
        state = [state[h] * egl[base + h] + _dot_tn(kd16[base + h], vnew16[h])
                 for h in range(MIX_HEADS)]
    for h in range(MIX_HEADS):
        s_ref[h] = state[h]


def _gdn(qkv, gates, *, batch, seq, ts):
    t = qkv.shape[0]
    nt = seq // ts
    kern = functools.partial(_gdn_kernel, ts=ts)

    def col(j):
        return pl.BlockSpec((ts, MIX_W), lambda b, s, j=j: (b * nt + s, j))

    return pl.pallas_call(
        kern,
        out_shape=jax.ShapeDtypeStruct((t, MIX_W), F32),
        grid=(batch, nt),
        in_specs=[col(0), col(1), col(2),
                  pl.BlockSpec((ts, GATE_COLS), lambda b, s: (b * nt + s, 0))],
        out_specs=pl.BlockSpec((ts, MIX_W), lambda b, s: (b * nt + s, 0)),
        scratch_shapes=[pltpu.VMEM((MIX_HEADS, HEAD_DIM, HEAD_DIM), F32)],
        compiler_params=pltpu.CompilerParams(dimension_semantics=("parallel", "arbitrary"),
                                             vmem_limit_bytes=VMEM_LIMIT),
        name="gdn",
    )(qkv, qkv, qkv, gates)


def _mlstm_kernel(q_ref, k_ref, v_ref, g_ref, o_ref, c_ref, m_ref, *, ts):
    @pl.when(pl.program_id(1) == 0)
    def _():
        c_ref[...] = jnp.zeros_like(c_ref)
        m_ref[...] = jnp.zeros_like(m_ref)

    L = CHUNK
    nch = ts // L
    g = g_ref[...]
    cs = _chunk_cumsum(g, ts)
    lane = lax.broadcasted_iota(jnp.int32, (L, GATE_COLS), 1)
    r = lax.broadcasted_iota(jnp.int32, (L, L), 0)
    c = lax.broadcasted_iota(jnp.int32, (L, L), 1)
    causal = r >= c
    ones = jnp.ones((L, HEAD_DIM), F32)

    items = [(ch, h) for ch in range(nch) for h in range(MIX_HEADS)]
    n = len(items)

    def rows_of(ch):
        return slice(ch * L, (ch + 1) * L)

    def cols_of(h):
        return slice(h * HEAD_DIM, (h + 1) * HEAD_DIM)

    q16 = [q_ref[rows_of(ch), cols_of(h)].astype(BF16) for ch, h in items]
    k = [k_ref[rows_of(ch), cols_of(h)] for ch, h in items]
    vaug = [jnp.concatenate([v_ref[rows_of(ch), cols_of(h)], ones], axis=1).astype(BF16)
            for ch, h in items]
    bc = [cs[rows_of(ch), 12 + h:13 + h] for ch, h in items]
    li = [g[rows_of(ch), 8 + h:9 + h] for ch, h in items]
    bl = [cs[(ch + 1) * L - 1:(ch + 1) * L, 12 + h:13 + h] for ch, h in items]
    dmat = []
    for ch, h in items:
        csch = cs[rows_of(ch)]
        gch = g[rows_of(ch)]
        e_f = jnp.where(lane == 12 + h, 1.0, 0.0)
        e_if = jnp.where(jnp.logical_or(lane == 8 + h, lane == 12 + h), 1.0, 0.0)
        xm = jnp.concatenate([_lane_pick(csch, lane, 12 + h), e_if], axis=1)
        ym = jnp.concatenate([e_f, _lane_pick(gch, lane, 8 + h) - _lane_pick(csch, lane, 12 + h)],
                             axis=1)
        dmat.append(jnp.where(causal, _dot_nt(xm, ym, precision=HIGHEST), NEG_BIG))
    qk = [_dot_nt(q16[i], k[i].astype(BF16)) for i in range(n)]
    rowmax = [jnp.max(d, axis=1, keepdims=True) for d in dmat]
    dl = [bl[i] - bc[i] + li[i] for i in range(n)]
    dlmax = [jnp.max(x, axis=0, keepdims=True) for x in dl]

    m_in = []
    m_cur = [m_ref[h][0:1, 0:1] for h in range(MIX_HEADS)]
    for i, (ch, h) in enumerate(items):
        m_in.append(m_cur[h])
        m_cur[h] = jnp.maximum(bl[i] + m_cur[h], dlmax[i])
    m_out = [m_cur[h] if ch == nch - 1 else m_in[(ch + 1) * MIX_HEADS + h] for ch, h in items]
    for h in range(MIX_HEADS):
        m_ref[h] = jnp.broadcast_to(m_cur[h], m_ref.shape[1:])

    inter = [bc[i] + m_in[i] for i in range(n)]
    mt = [jnp.maximum(inter[i], rowmax[i]) for i in range(n)]
    p16 = [(qk[i] * jnp.exp(dmat[i] - mt[i])).astype(BF16) for i in range(n)]
    sc = [jnp.exp(inter[i] - mt[i]) for i in range(n)]
    pv = [_dot(p16[i], vaug[i]) for i in range(n)]
    kw16 = [(k[i] * jnp.exp(dl[i] - m_out[i])).astype(BF16) for i in range(n)]
    kv = [_dot_tn(kw16[i], vaug[i]) for i in range(n)]
    dec = [jnp.exp(bl[i] + m_in[i] - m_out[i]) for i in range(n)]

    cst = [c_ref[h] for h in range(MIX_HEADS)]
    c_in = []
    for i, (ch, h) in enumerate(items):
        c_in.append(cst[h].astype(BF16))
        cst[h] = dec[i] * cst[h] + kv[i]
    for h in range(MIX_HEADS):
        c_ref[h] = cst[h]

    for i, (ch, h) in enumerate(items):
        nd = sc[i] * _dot(q16[i], c_in[i]) + pv[i]
        den = jnp.maximum(jnp.abs(nd[:, HEAD_DIM:]), jnp.exp(-mt[i]))
        o_ref[rows_of(ch), cols_of(h)] = nd[:, :HEAD_DIM] / den


def _mlstm(qkv, zvo, gates, *, batch, seq, ts):
    t = qkv.shape[0]
    nt = seq // ts
    kern = functools.partial(_mlstm_kernel, ts=ts)

    def col(j):
        return pl.BlockSpec((ts, MIX_W), lambda b, s, j=j: (b * nt + s, j))

    return pl.pallas_call(
        kern,
        out_shape=jax.ShapeDtypeStruct((t, MIX_W), F32),
        grid=(batch, nt),
        in_specs=[col(3), col(4), col(1),
                  pl.BlockSpec((ts, GATE_COLS), lambda b, s: (b * nt + s, 0))],
        out_specs=pl.BlockSpec((ts, MIX_W), lambda b, s: (b * nt + s, 0)),
        scratch_shapes=[pltpu.VMEM((MIX_HEADS, HEAD_DIM, 2 * HEAD_DIM), F32),
                        pltpu.VMEM((MIX_HEADS, 8, 128), F32)],
        compiler_params=pltpu.CompilerParams(dimension_semantics=("parallel", "arbitrary"),
                                             vmem_limit_bytes=VMEM_LIMIT),
        name="mlstm",
    )(qkv, qkv, zvo, gates)


def _mix_xattn_kernel(x_ref, go_ref, mh_ref, z_ref, mo_ref, gnw_ref, mnw_ref, wout_ref,
                      nxa_ref, wq_ref, k_ref, v_ref, wo_ref, out_ref):
    parts = []
    gnw = gnw_ref[...]
    for h in range(MIX_HEADS):
        cols = slice(h * HEAD_DIM, (h + 1) * HEAD_DIM)
        z = z_ref[:, cols]
        parts.append(_rms(go_ref[:, cols], gnw) * (z * _sigmoid(z)))
    for h in range(MIX_HEADS):
        cols = slice(h * HEAD_DIM, (h + 1) * HEAD_DIM)
        parts.append(_rms(mh_ref[:, cols], mnw_ref[:, cols]) * _sigmoid(mo_ref[:, cols]))
    mixed = jnp.concatenate(parts, axis=1).astype(BF16)
    x1 = x_ref[...] + _dot(mixed, wout_ref[...])

    xn = _rms(x1, nxa_ref[...]).astype(BF16)
    q = _dot(xn, wq_ref[...])
    kk = k_ref[0]
    vv = v_ref[0]
    outs = []
    for h in range(XA_HEADS):
        cols = slice(h * XA_HEAD_DIM, (h + 1) * XA_HEAD_DIM)
        s = _dot_nt(q[:, cols].astype(BF16), kk[:, cols]) * (XA_HEAD_DIM ** -0.5)
        s = s - jnp.max(s, axis=-1, keepdims=True)
        e = jnp.exp(s)
        p = e / jnp.sum(e, axis=-1, keepdims=True)
        outs.append(_dot(p.astype(BF16), vv[:, cols]))
    o = jnp.concatenate(outs, axis=1).astype(BF16)
    out_ref[...] = x1 + _dot(o, wo_ref[...])


def _mix_xattn(x2d, go, mh, zvo, gnw, mnw, wout, nxa, wq, kmem, vmem, wo, *, seq, tm):
    t = x2d.shape[0]
    per_seq = seq // tm

    def row(width, j=0):
        return pl.BlockSpec((tm, width), lambda i, j=j: (i, j))

    def const(shape):
        return pl.BlockSpec(shape, lambda i: (0,) * len(shape))

    mem_spec = pl.BlockSpec((1, MEM_TOKENS, D_MODEL), lambda i: (i // per_seq, 0, 0))
    return pl.pallas_call(
        _mix_xattn_kernel,
        out_shape=jax.ShapeDtypeStruct((t, D_MODEL), F32),
        grid=(t // tm,),
        in_specs=[row(D_MODEL), row(MIX_W), row(MIX_W), row(MIX_W, 0), row(MIX_W, 2),
                  const((1, HEAD_DIM)), const((1, MIX_W)), const((D_MODEL, D_MODEL)),
                  const((1, D_MODEL)), const((D_MODEL, D_MODEL)), mem_spec, mem_spec,
                  const((D_MODEL, D_MODEL))],
        out_specs=row(D_MODEL),
        compiler_params=pltpu.CompilerParams(dimension_semantics=("parallel",),
                                             vmem_limit_bytes=VMEM_LIMIT),
        name="mix_xattn",
    )(x2d, go, mh, zvo, zvo, gnw, mnw, wout, nxa, wq, kmem, vmem, wo)


def _take_top16(values, order, exact):
    tt = values.shape[1]
    slot = lax.broadcasted_iota(jnp.int32, (PEER_TOPK, tt), 0)
    v = values
    step = jnp.full(values.shape, float(PEER_TOPK), F32)
    vals = jnp.zeros((PEER_TOPK, tt), F32)
    for kk in range(PEER_TOPK):
        m = jnp.max(v, axis=0, keepdims=True)
        if exact:
            first = jnp.min(jnp.where(v == m, order, 1e9), axis=0, keepdims=True)
            sel = order == first
        else:
            sel = v == m
        step = jnp.where(sel, float(kk), step)
        v = jnp.where(sel, -jnp.inf, v)
        vals = jnp.where(slot == kk, m, vals)
    count = jnp.sum(jnp.where(step < float(PEER_TOPK), 1.0, 0.0), axis=0, keepdims=True)
    return step, vals, count


def _candidate_pairs(a, b):
    tt = a.shape[1]
    sub = lax.broadcasted_iota(jnp.int32, (8, 1), 0)
    zero = jnp.zeros((8, 1), jnp.int32)

    def row(x, p):
        return jnp.broadcast_to(x[p:p + 1], (8, tt))

    b_lo4 = jnp.where(sub < 4, b[0:8], pltpu.roll(b[0:8], 4, 0))
    b_01 = jnp.where(sub % 2 == 0, row(b, 0), row(b, 1))
    groups = (
        (row(a, 0), b[0:8], zero, sub, sub < 8),
        (row(a, 0), b[8:16], zero, sub + 8, sub < 8),
        (row(a, 1), b[0:8], zero + 1, sub, sub < 8),
        (row(a, 2), b[0:8], zero + 2, sub, sub < 5),
        (jnp.where(sub < 4, row(a, 3), row(a, 4)), b_lo4,
         jnp.where(sub < 4, 3, 4), sub % 4, sub < 7),
        (jnp.where(sub < 2, row(a, 5), jnp.where(sub < 4, row(a, 6), row(a, 7))), b_01,
         5 + sub // 2, sub % 2, sub < 6),
        (a[8:16], row(b, 0), sub + 8, zero, sub < 8),
    )
    cand = jnp.concatenate([g[0] + g[1] for g in groups], axis=0)
    pidx = jnp.concatenate([g[2] for g in groups], axis=0)
    qidx = jnp.concatenate([g[3] for g in groups], axis=0)
    valid = jnp.concatenate([g[4] for g in groups], axis=0)
    return cand, pidx, qidx, valid


def _route_kernel(x_ref, nw_ref, wq_ref, keys_ref, xnt_ref, c0_ref, ap_ref, r1_ref, bp_ref,
                  xn_scr, *, tt):
    @pl.when(pl.program_id(1) == 0)
    def _():
        xn32 = _rms(x_ref[...], nw_ref[...])
        xn_scr[...] = xn32.astype(BF16)
        xnt_ref[...] = xn32.T.astype(BF16)

    xn = xn_scr[...]
    q = _dot(xn, wq_ref[...])
    s0 = _dot_nt(keys_ref[0, 0], q[:, :PEER_HALF].astype(BF16))
    s1 = _dot_nt(keys_ref[0, 1], q[:, PEER_HALF:].astype(BF16))

    def tables(exact):
        kidx = lax.broadcasted_iota(jnp.int32, s0.shape, 0).astype(F32)
        r0, a, n0 = _take_top16(s0, kidx, exact)
        r1, b, n1 = _take_top16(s1, kidx, exact)

        cand, pidx, qidx, valid = _candidate_pairs(a, b)
        pos = jnp.broadcast_to((pidx * PEER_TOPK + qidx).astype(F32), cand.shape)
        taken, _, n2 = _take_top16(jnp.where(valid, cand, -jnp.inf), pos, exact)
        chosen = jnp.where(jnp.logical_and(valid, taken < float(PEER_TOPK)), 1.0, 0.0)

        top = a[0:1] + b[0:1]
        gsel = chosen * jnp.exp(jnp.where(valid, cand, top) - top)
        zinv = 1.0 / jnp.sum(gsel, axis=0, keepdims=True)

        c0 = jnp.zeros(r0.shape, F32)
        for p in range(PEER_TOPK):
            cnt = jnp.sum(jnp.where(pidx == p, chosen, 0.0), axis=0, keepdims=True)
            c0 = jnp.where(r0 == float(p), cnt, c0)

        c0_ref[0] = c0
        ap_ref[0] = jnp.exp(s0 - a[0:1]) * (zinv * math.sqrt(0.5))
        r1_ref[0] = (r1 * RANK_SCALE).astype(BF16)
        bp_ref[0] = jnp.exp(s1 - b[0:1]).astype(BF16)
        return n0 + n1 + n2

    taken_total = tables(exact=False)
    tied = jnp.max(jnp.abs(taken_total - 3.0 * PEER_TOPK)) > 0.0

    @pl.when(tied)
    def _():
        tables(exact=True)


def _peer_route(x2, nw, wq, keys, *, tt):
    t = x2.shape[0]
    kern = functools.partial(_route_kernel, tt=tt)
    tab = lambda dt: jax.ShapeDtypeStruct((PEER_HEADS, PEER_KEYS, t), dt)
    tab_spec = pl.BlockSpec((1, PEER_KEYS, tt), lambda i, h: (h, 0, i))
    return pl.pallas_call(
        kern,
        out_shape=(jax.ShapeDtypeStruct((D_MODEL, t), BF16), tab(F32), tab(F32), tab(BF16), tab(BF16)),
        grid=(t // tt, PEER_HEADS),
        in_specs=[pl.BlockSpec((tt, D_MODEL), lambda i, h: (i, 0)),
                  pl.BlockSpec((1, D_MODEL), lambda i, h: (0, 0)),
                  pl.BlockSpec((D_MODEL, 2 * PEER_HALF), lambda i, h: (0, h)),
                  pl.BlockSpec((1, 2, PEER_KEYS, PEER_HALF), lambda i, h: (h, 0, 0, 0))],
        out_specs=(pl.BlockSpec((D_MODEL, tt), lambda i, h: (0, i)),
                   tab_spec, tab_spec, tab_spec, tab_spec),
        scratch_shapes=[pltpu.VMEM((tt, D_MODEL), BF16)],
        compiler_params=pltpu.CompilerParams(dimension_semantics=("parallel", "arbitrary"),
                                             vmem_limit_bytes=VMEM_LIMIT),
        name="peer_route",
    )(x2, nw, wq, keys)


def _peer_eval_kernel(xnt_ref, x_ref, u_ref, v_ref, c0_ref, ap_ref, r1_ref, bp_ref, fnw_ref,
                      out_ref, acc_ref, w_scr, *, eb, sb):
    e = pl.program_id(1)

    @pl.when(e == 0)
    def _():
        acc_ref[...] = jnp.zeros_like(acc_ref)

    tt = xnt_ref.shape[1]
    nsub = eb // sb
    ipb = sb // PEER_KEYS

    def activations(s):
        return _dot(u_ref[s * sb:(s + 1) * sb, :], xnt_ref[...])

    def row_tile(ref, h, il, lanes, scale):
        row = ref[h, il:il + 1, lanes] * scale
        return jnp.broadcast_to(row, (BF16_ROWS, LANES)).astype(BF16)[None]

    def weight_slab(s, j):
        ii, lg = divmod(j, tt // LANES)
        il = s * ipb + ii
        lanes = slice(lg * LANES, (lg + 1) * LANES)
        w = None
        for h in range(PEER_HEADS):
            gate = jnp.clip(row_tile(c0_ref, h, il, lanes, RANK_SCALE) - r1_ref[h, :, :, lanes],
                            0.0, row_tile(ap_ref, h, il, lanes, 1.0))
            term = gate * bp_ref[h, :, :, lanes]
            w = term if w is None else w + term
        w_scr[ii * PEER_KEYS:(ii + 1) * PEER_KEYS, lanes] = w.reshape(PEER_KEYS, LANES)

    act = activations(0)
    part = None
    for s in range(nsub):
        act_next = activations(s + 1) if s + 1 < nsub else None
        for j in range(ipb * (tt // LANES)):
            weight_slab(s, j)
        z = act.astype(BF16)
        pt = w_scr[...] * (z * (1.0 + lax.erf(z)))
        d = _dot_tn(pt, v_ref[s * sb:(s + 1) * sb, :])
        part = d if part is None else part + d
        act = act_next
    acc_ref[...] += part

    @pl.when(e == pl.num_programs(1) - 1)
    def _():
        out_ref[...] = _rms(x_ref[...] + acc_ref[...], fnw_ref[...])


def _peer_eval(xn, x2, u16, v16, c0, ap, r1, bp, fnw, *, tt, eb, sb):
    t = x2.shape[0]
    ne = u16.shape[0]
    ib = eb // PEER_KEYS
    kern = functools.partial(_peer_eval_kernel, eb=eb, sb=sb)
    half0 = pl.BlockSpec((PEER_HEADS, ib, tt), lambda i, e: (0, e, i))
    ktiles = PEER_KEYS // BF16_ROWS
    r1 = r1.reshape(PEER_HEADS, ktiles, BF16_ROWS, t)
    bp = bp.reshape(PEER_HEADS, ktiles, BF16_ROWS, t)
    half1 = pl.BlockSpec((PEER_HEADS, ktiles, BF16_ROWS, tt), lambda i, e: (0, 0, 0, i))
    return pl.pallas_call(
        kern,
        out_shape=jax.ShapeDtypeStruct((t, D_MODEL), F32),
        grid=(t // tt, ne // eb),
        in_specs=[pl.BlockSpec((D_MODEL, tt), lambda i, e: (0, i)),
                  pl.BlockSpec((tt, D_MODEL), lambda i, e: (i, 0)),
                  pl.BlockSpec((eb, D_MODEL), lambda i, e: (e, 0)),
                  pl.BlockSpec((eb, D_MODEL), lambda i, e: (e, 0)),
                  half0, half0, half1, half1,
                  pl.BlockSpec((1, D_MODEL), lambda i, e: (0, 0))],
        out_specs=pl.BlockSpec((tt, D_MODEL), lambda i, e: (i, 0)),
        scratch_shapes=[pltpu.VMEM((tt, D_MODEL), F32), pltpu.VMEM((sb, tt), BF16)],
        compiler_params=pltpu.CompilerParams(dimension_semantics=("parallel", "arbitrary"),
                                             vmem_limit_bytes=VMEM_LIMIT),
        name="peer_eval",
    )(xn, x2, u16, v16, c0, ap, r1, bp, fnw)


def kernel(x, mem, norm_mix_w, w_in, gdn_conv_w, gdn_a_log, gdn_dt_bias, gdn_norm_w, mlstm_conv_w,
           mlstm_i_bias, mlstm_f_bias, mlstm_norm_w, w_out, norm_xa_w, norm_mem_w, xa_wq, xa_wkv,
           xa_wo, norm_ffn_w, peer_wq, peer_sub_keys, peer_u, peer_v, norm_final_w):
    batch, seq, d = x.shape
    t = batch * seq
    assert d == D_MODEL and w_in.shape[0] == 1, "single-layer block; the final norm is fused into it"
    tm = min(256, seq)
    ts = min(256, seq)
    tt = min(512, t)
    xs = x.reshape(t, d)
    mem2d = mem.reshape(batch * MEM_TOKENS, d)

    wl = w_in[0]
    o_gz = 3 * MIX_W
    o_ga = o_gz + MIX_W
    o_gb = o_ga + MIX_HEADS
    o_mqk = o_gb + MIX_HEADS
    o_mv = o_mqk + 2 * MIX_W
    o_mo = o_mv + MIX_W
    o_mi = o_mo + MIX_W
    o_mf = o_mi + MIX_HEADS
    w_all = jnp.concatenate([wl[:, 0:o_gz], wl[:, o_mqk:o_mv], wl[:, o_gz:o_ga],
                             wl[:, o_mv:o_mo], wl[:, o_mo:o_mi],
                             wl[:, o_ga:o_mqk], wl[:, o_mi:o_mf + MIX_HEADS],
                             jnp.zeros((d, GATE_COLS - 4 * MIX_HEADS), wl.dtype)], axis=1).astype(BF16)
    cw = jnp.concatenate([gdn_conv_w[0], mlstm_conv_w[0]], axis=1).astype(F32)
    zeros4 = jnp.zeros((MIX_HEADS,), F32)
    pad = jnp.zeros((GATE_COLS - 4 * MIX_HEADS,), F32)
    gbias = jnp.concatenate([gdn_dt_bias[0], zeros4, mlstm_i_bias[0], mlstm_f_bias[0], pad]
                            ).astype(F32).reshape(1, GATE_COLS)
    alog = jnp.concatenate([gdn_a_log[0], zeros4, zeros4, zeros4, pad]).astype(F32).reshape(1, GATE_COLS)

    kmem, vmem = _memkv(mem2d, norm_mem_w[0].reshape(1, d), xa_wkv[0].astype(BF16))
    qkv, zvo, gates = _inproj(xs, norm_mix_w[0].reshape(1, d), w_all, cw, gbias, alog,
                              seq=seq, tm=tm)
    go = _gdn(qkv, gates, batch=batch, seq=seq, ts=ts)
    mh = _mlstm(qkv, zvo, gates, batch=batch, seq=seq, ts=ts)
    x2 = _mix_xattn(xs, go, mh, zvo, gdn_norm_w[0].reshape(1, HEAD_DIM),
                    mlstm_norm_w[0].reshape(1, MIX_W), w_out[0].astype(BF16),
                    norm_xa_w[0].reshape(1, d), xa_wq[0].astype(BF16),
                    kmem.reshape(batch, MEM_TOKENS, d), vmem.reshape(batch, MEM_TOKENS, d),
                    xa_wo[0].astype(BF16), seq=seq, tm=tm)
    xn, c0, ap, r1, bp = _peer_route(x2, norm_ffn_w[0].reshape(1, d), peer_wq[0].astype(BF16),
                                     peer_sub_keys[0].astype(BF16), tt=tt)
    u16 = (peer_u[0] * math.sqrt(0.5)).astype(BF16)
    out = _peer_eval(xn, x2, u16, peer_v[0].astype(BF16), c0, ap, r1, bp,
                     norm_final_w.reshape(1, d), tt=tt, eb=4096, sb=256)
    return out.reshape(batch, seq, d)
```

```python
import functools
import math

import jax
import jax.numpy as jnp
from jax import lax
from jax.experimental import pallas as pl
from jax.experimental.pallas import tpu as pltpu

F32 = jnp.float32
BF16 = jnp.bfloat16
HIGHEST = lax.Precision.HIGHEST

D_MODEL = 1024
HEAD_DIM = 128
MIX_HEADS = 4
MIX_W = MIX_HEADS * HEAD_DIM
CONV_WIDTH = 4
CHUNK = 64
MEM_TOKENS = 256
XA_HEADS = 4
XA_HEAD_DIM = D_MODEL // XA_HEADS
PEER_HEADS = 8
PEER_KEYS = 128
PEER_TOPK = 16
PEER_HALF = 128
NORM_EPS = 1e-6
NEG_BIG = -1e30
LANES = 128
MXU_COLS = 256
BF16_ROWS = 16
RANK_SCALE = 2.0

CONV_COLS = 3 * MIX_W + 2 * MIX_W
PLAIN_COLS = 3 * MIX_W
GATE_COLS = 128
PROJ_COLS = CONV_COLS + PLAIN_COLS + GATE_COLS
HALO = 8

VMEM_LIMIT = 60 * 1024 * 1024

NT_DIMS = (((1,), (1,)), ((), ()))
TN_DIMS = (((0,), (0,)), ((), ()))


def _rms(x, w):
    return x * lax.rsqrt(jnp.mean(x * x, axis=-1, keepdims=True) + NORM_EPS) * w


def _sigmoid(x):
    return 1.0 / (1.0 + jnp.exp(-x))


def _softplus(x):
    return jnp.maximum(x, 0.0) + jnp.log1p(jnp.exp(-jnp.abs(x)))


def _dot(a, b):
    return jnp.dot(a, b, preferred_element_type=F32)


def _dot_nt(a, b, precision=None):
    return lax.dot_general(a, b, NT_DIMS, precision=precision, preferred_element_type=F32)


def _dot_tn(a, b):
    return lax.dot_general(a, b, TN_DIMS, preferred_element_type=F32)


def _memkv_kernel(m_ref, nw_ref, w_ref, k_ref, v_ref):
    mn = _rms(m_ref[...], nw_ref[...]).astype(BF16)
    kv = _dot(mn, w_ref[...])
    k_ref[...] = kv[:, :D_MODEL].astype(BF16)
    v_ref[...] = kv[:, D_MODEL:].astype(BF16)


def _memkv(mem2d, nw, wkv):
    n = mem2d.shape[0]
    tm = MEM_TOKENS
    return pl.pallas_call(
        _memkv_kernel,
        out_shape=(jax.ShapeDtypeStruct((n, D_MODEL), BF16),
                   jax.ShapeDtypeStruct((n, D_MODEL), BF16)),
        grid=(n // tm,),
        in_specs=[pl.BlockSpec((tm, D_MODEL), lambda i: (i, 0)),
                  pl.BlockSpec((1, D_MODEL), lambda i: (0, 0)),
                  pl.BlockSpec((D_MODEL, 2 * D_MODEL), lambda i: (0, 0))],
        out_specs=(pl.BlockSpec((tm, D_MODEL), lambda i: (i, 0)),
                   pl.BlockSpec((tm, D_MODEL), lambda i: (i, 0))),
        compiler_params=pltpu.CompilerParams(dimension_semantics=("parallel",),
                                             vmem_limit_bytes=VMEM_LIMIT),
        name="memkv",
    )(mem2d, nw, wkv)


def _inproj_kernel(x_ref, halo_ref, nw_ref, w_ref, cw_ref, gbias_ref, alog_ref,
                   qkv_ref, zvo_ref, gate_ref, p_scr, *, tm, seq):
    i = pl.program_id(0)
    xc = jnp.concatenate([halo_ref[...], x_ref[...]], axis=0)
    xn = _rms(xc, nw_ref[...]).astype(BF16)
    p = _dot(xn, w_ref[...])

    first_row = jnp.where((i * tm) % seq == 0, HALO, 0)
    row = lax.broadcasted_iota(jnp.int32, (HALO, 1), 0)
    p_scr[:HALO, :] = jnp.where(row >= first_row, p[:HALO, :CONV_COLS], 0.0)
    p_scr[HALO:, :] = p[HALO:, :CONV_COLS]
    cw = cw_ref[...]
    y = cw[CONV_WIDTH - 1:CONV_WIDTH] * p_scr[HALO:HALO + tm, :]
    for kk in range(CONV_WIDTH - 1):
        off = HALO - (CONV_WIDTH - 1) + kk
        y = y + cw[kk:kk + 1] * p_scr[off:off + tm, :]
    y = y * _sigmoid(y)

    scale = HEAD_DIM ** -0.5
    for hh in range(2 * MIX_HEADS):
        cols = slice(hh * HEAD_DIM, (hh + 1) * HEAD_DIM)
        yh = y[:, cols]
        inv = lax.rsqrt(jnp.sum(yh * yh, axis=-1, keepdims=True) + NORM_EPS)
        if hh < MIX_HEADS:
            inv = inv * scale
        qkv_ref[:, cols] = yh * inv
    qkv_ref[:, 2 * MIX_W:4 * MIX_W] = y[:, 2 * MIX_W:4 * MIX_W]
    qkv_ref[:, 4 * MIX_W:5 * MIX_W] = y[:, 4 * MIX_W:5 * MIX_W] * scale

    zvo_ref[...] = p[HALO:, CONV_COLS:CONV_COLS + PLAIN_COLS]

    ga = p[HALO:, CONV_COLS + PLAIN_COLS:] + gbias_ref[...]
    lane = lax.broadcasted_iota(jnp.int32, (tm, GATE_COLS), 1)
    log_alpha = -jnp.exp(alog_ref[...]) * _softplus(ga)
    beta = _sigmoid(ga)
    log_f = -_softplus(-ga)
    gate_ref[...] = jnp.where(lane < 4, log_alpha,
                              jnp.where(lane < 8, beta,
                                        jnp.where(lane < 12, ga,
                                                  jnp.where(lane < 16, log_f, 0.0))))


def _inproj(x2d, nw, w_all, cw, gbias, alog, *, seq, tm):
    t = x2d.shape[0]
    hb = tm // HALO
    kern = functools.partial(_inproj_kernel, tm=tm, seq=seq)
    return pl.pallas_call(
        kern,
        out_shape=(jax.ShapeDtypeStruct((t, CONV_COLS), F32),
                   jax.ShapeDtypeStruct((t, PLAIN_COLS), F32),
                   jax.ShapeDtypeStruct((t, GATE_COLS), F32)),
        grid=(t // tm,),
        in_specs=[pl.BlockSpec((tm, D_MODEL), lambda i: (i, 0)),
                  pl.BlockSpec((HALO, D_MODEL), lambda i: (jnp.maximum(i * hb - 1, 0), 0)),
                  pl.BlockSpec((1, D_MODEL), lambda i: (0, 0)),
                  pl.BlockSpec((D_MODEL, PROJ_COLS), lambda i: (0, 0)),
                  pl.BlockSpec((CONV_WIDTH, CONV_COLS), lambda i: (0, 0)),
                  pl.BlockSpec((1, GATE_COLS), lambda i: (0, 0)),
                  pl.BlockSpec((1, GATE_COLS), lambda i: (0, 0))],
        out_specs=(pl.BlockSpec((tm, CONV_COLS), lambda i: (i, 0)),
                   pl.BlockSpec((tm, PLAIN_COLS), lambda i: (i, 0)),
                   pl.BlockSpec((tm, GATE_COLS), lambda i: (i, 0))),
        scratch_shapes=[pltpu.VMEM((HALO + tm, CONV_COLS), F32)],
        compiler_params=pltpu.CompilerParams(dimension_semantics=("parallel",),
                                             vmem_limit_bytes=VMEM_LIMIT),
        name="inproj",
    )(x2d, x2d, nw, w_all, cw, gbias, alog)


def _chunk_cumsum(g, ts):
    ri = lax.broadcasted_iota(jnp.int32, (ts, ts), 0)
    ci = lax.broadcasted_iota(jnp.int32, (ts, ts), 1)
    shift = CHUNK.bit_length() - 1
    same_chunk = jnp.right_shift(ri, shift) == jnp.right_shift(ci, shift)
    tri = jnp.where(jnp.logical_and(same_chunk, ci <= ri), 1.0, 0.0)
    return jnp.dot(tri, g, precision=HIGHEST, preferred_element_type=F32)


def _lane_pick(x, lane, idx):
    return jnp.where(lane == idx, x, 0.0)


def _gdn_kernel(q_ref, k_ref, v_ref, g_ref, o_ref, s_ref, *, ts):
    @pl.when(pl.program_id(1) == 0)
    def _():
        s_ref[...] = jnp.zeros_like(s_ref)

    L = CHUNK
    nch = ts // L
    g = g_ref[...]
    cs = _chunk_cumsum(g, ts)
    lane = lax.broadcasted_iota(jnp.int32, (L, GATE_COLS), 1)
    r = lax.broadcasted_iota(jnp.int32, (L, L), 0)
    c = lax.broadcasted_iota(jnp.int32, (L, L), 1)
    causal = r >= c
    strict = r > c
    eye = jnp.where(r == c, 1.0, 0.0)

    items = [(ch, h) for ch in range(nch) for h in range(MIX_HEADS)]
    n = len(items)

    def rows_of(ch):
        return slice(ch * L, (ch + 1) * L)

    def cols_of(h):
        return slice(h * HEAD_DIM, (h + 1) * HEAD_DIM)

    q = [q_ref[rows_of(ch), cols_of(h)] for ch, h in items]
    k = [k_ref[rows_of(ch), cols_of(h)] for ch, h in items]
    v = [v_ref[rows_of(ch), cols_of(h)] for ch, h in items]
    gc = [cs[rows_of(ch), h:h + 1] for ch, h in items]
    beta = [g[rows_of(ch), 4 + h:5 + h] for ch, h in items]
    gl = [cs[(ch + 1) * L - 1:(ch + 1) * L, h:h + 1] for ch, h in items]

    diff = []
    for ch, h in items:
        e_h = jnp.where(lane == h, 1.0, 0.0)
        cs_h = _lane_pick(cs[rows_of(ch)], lane, h)
        diff.append(_dot_nt(jnp.concatenate([cs_h, e_h], axis=1),
                            jnp.concatenate([e_h, -cs_h], axis=1), precision=HIGHEST))
    decay = [jnp.exp(jnp.where(causal, d, NEG_BIG)) for d in diff]
    eg = [jnp.exp(x) for x in gc]
    kb = [k[i] * beta[i] for i in range(n)]
    k16 = [x.astype(BF16) for x in k]
    q16 = [x.astype(BF16) for x in q]
    kk = [_dot_nt(kb[i].astype(BF16), k16[i]) for i in range(n)]
    qk = [_dot_nt(q16[i], k16[i]) for i in range(n)]
    a = [jnp.where(strict, kk[i] * decay[i], 0.0) for i in range(n)]
    qk16 = [(qk[i] * decay[i]).astype(BF16) for i in range(n)]
    tinv = [eye - x for x in a]
    pw = a
    for _ in range(5):
        pw16 = [x.astype(BF16) for x in pw]
        pw = [_dot(x, x) for x in pw16]
        tinv = [tinv[i] + _dot(tinv[i].astype(BF16), pw[i].astype(BF16)) for i in range(n)]
    rhs = [jnp.concatenate([v[i] * beta[i], kb[i] * eg[i]], axis=1).astype(BF16) for i in range(n)]
    uw = [_dot(tinv[i].astype(BF16), rhs[i]) for i in range(n)]
    wq16 = [jnp.concatenate([uw[i][:, HEAD_DIM:], q[i] * eg[i]], axis=0).astype(BF16)
            for i in range(n)]
    kd16 = [(k[i] * jnp.exp(gl[i] - gc[i])).astype(BF16) for i in range(n)]
    egl = [jnp.exp(x) for x in gl]

    state = [s_ref[h] for h in range(MIX_HEADS)]
    for ch in range(nch):
        base = ch * MIX_HEADS
        rr = [_dot(wq16[base + h], state[h].astype(BF16)) for h in range(MIX_HEADS)]
        vnew16 = [(uw[base + h][:, :HEAD_DIM] - rr[h][:L]).astype(BF16) for h in range(MIX_HEADS)]
        for h in range(MIX_HEADS):
            o_ref[rows_of(ch), cols_of(h)] = rr[h][L:] + _dot(qk16[base + h], vnew16[h])
        state = [state[h] * egl[base + h] + _dot_tn(kd16[base + h], vnew16[h])
                 for h in range(MIX_HEADS)]
    for h in range(MIX_HEADS):
        s_ref[h] = state[h]


def _gdn(qkv, gates, *, batch, seq, ts):
    t = qkv.shape[0]
    nt = seq // ts
    kern = functools.partial(_gdn_kernel, ts=ts)

    def col(j):
        return pl.BlockSpec((ts, MIX_W), lambda b, s, j=j: (b * nt + s, j))

    return pl.pallas_call(
        kern,
        out_shape=jax.ShapeDtypeStruct((t, MIX_W), F32),
        grid=(batch, nt),
        in_specs=[col(0), col(1), col(2),
                  pl.BlockSpec((ts, GATE_COLS), lambda b, s: (b * nt + s, 0))],
        out_specs=pl.BlockSpec((ts, MIX_W), lambda b, s: (b * nt + s, 0)),
        scratch_shapes=[pltpu.VMEM((MIX_HEADS, HEAD_DIM, HEAD_DIM), F32)],
        compiler_params=pltpu.CompilerParams(dimension_semantics=("parallel", "arbitrary"),
                                             vmem_limit_bytes=VMEM_LIMIT),
        name="gdn",
    )(qkv, qkv, qkv, gates)


def _mlstm_kernel(q_ref, k_ref, v_ref, g_ref, o_ref, c_ref, m_ref, *, ts):
    @pl.when(pl.program_id(1) == 0)
    def _():
        c_ref[...] = jnp.zeros_like(c_ref)
        m_ref[...] = jnp.zeros_like(m_ref)

    L = CHUNK
    nch = ts // L
    g = g_ref[...]
    cs = _chunk_cumsum(g, ts)
    lane = lax.broadcasted_iota(jnp.int32, (L, GATE_COLS), 1)
    r = lax.broadcasted_iota(jnp.int32, (L, L), 0)
    c = lax.broadcasted_iota(jnp.int32, (L, L), 1)
    causal = r >= c
    ones = jnp.ones((L, HEAD_DIM), F32)

    items = [(ch, h) for ch in range(nch) for h in range(MIX_HEADS)]
    n = len(items)

    def rows_of(ch):
        return slice(ch * L, (ch + 1) * L)

    def cols_of(h):
        return slice(h * HEAD_DIM, (h + 1) * HEAD_DIM)

    q16 = [q_ref[rows_of(ch), cols_of(h)].astype(BF16) for ch, h in items]
    k = [k_ref[rows_of(ch), cols_of(h)] for ch, h in items]
    vaug = [jnp.concatenate([v_ref[rows_of(ch), cols_of(h)], ones], axis=1).astype(BF16)
            for ch, h in items]
    bc = [cs[rows_of(ch), 12 + h:13 + h] for ch, h in items]
    li = [g[rows_of(ch), 8 + h:9 + h] for ch, h in items]
    bl = [cs[(ch + 1) * L - 1:(ch + 1) * L, 12 + h:13 + h] for ch, h in items]
    dmat = []
    for ch, h in items:
        csch = cs[rows_of(ch)]
        gch = g[rows_of(ch)]
        e_f = jnp.where(lane == 12 + h, 1.0, 0.0)
        e_if = jnp.where(jnp.logical_or(lane == 8 + h, lane == 12 + h), 1.0, 0.0)
        xm = jnp.concatenate([_lane_pick(csch, lane, 12 + h), e_if], axis=1)
        ym = jnp.concatenate([e_f, _lane_pick(gch, lane, 8 + h) - _lane_pick(csch, lane, 12 + h)],
                             axis=1)
        dmat.append(jnp.where(causal, _dot_nt(xm, ym, precision=HIGHEST), NEG_BIG))
    qk = [_dot_nt(q16[i], k[i].astype(BF16)) for i in range(n)]
    rowmax = [jnp.max(d, axis=1, keepdims=True) for d in dmat]
    dl = [bl[i] - bc[i] + li[i] for i in range(n)]
    dlmax = [jnp.max(x, axis=0, keepdims=True) for x in dl]

    m_in = []
    m_cur = [m_ref[h][0:1, 0:1] for h in range(MIX_HEADS)]
    for i, (ch, h) in enumerate(items):
        m_in.append(m_cur[h])
        m_cur[h] = jnp.maximum(bl[i] + m_cur[h], dlmax[i])
    m_out = [m_cur[h] if ch == nch - 1 else m_in[(ch + 1) * MIX_HEADS + h] for ch, h in items]
    for h in range(MIX_HEADS):
        m_ref[h] = jnp.broadcast_to(m_cur[h], m_ref.shape[1:])

    inter = [bc[i] + m_in[i] for i in range(n)]
    mt = [jnp.maximum(inter[i], rowmax[i]) for i in range(n)]
    p16 = [(qk[i] * jnp.exp(dmat[i] - mt[i])).astype(BF16) for i in range(n)]
    sc = [jnp.exp(inter[i] - mt[i]) for i in range(n)]
    pv = [_dot(p16[i], vaug[i]) for i in range(n)]
    kw16 = [(k[i] * jnp.exp(dl[i] - m_out[i])).astype(BF16) for i in range(n)]
    kv = [_dot_tn(kw16[i], vaug[i]) for i in range(n)]
    dec = [jnp.exp(bl[i] + m_in[i] - m_out[i]) for i in range(n)]

    cst = [c_ref[h] for h in range(MIX_HEADS)]
    c_in = []
    for i, (ch, h) in enumerate(items):
        c_in.append(cst[h].astype(BF16))
        cst[h] = dec[i] * cst[h] + kv[i]
    for h in range(MIX_HEADS):
        c_ref[h] = cst[h]

    for i, (ch, h) in enumerate(items):
        nd = sc[i] * _dot(q16[i], c_in[i]) + pv[i]
        den = jnp.maximum(jnp.abs(nd[:, HEAD_DIM:]), jnp.exp(-mt[i]))
        o_ref[rows_of(ch), cols_of(h)] = nd[:, :HEAD_DIM] / den


def _mlstm(qkv, zvo, gates, *, batch, seq, ts):
    t = qkv.shape[0]
    nt = seq // ts
    kern = functools.partial(_mlstm_kernel, ts=ts)

    def col(j):
        return pl.BlockSpec((ts, MIX_W), lambda b, s, j=j: (b * nt + s, j))

    return pl.pallas_call(
        kern,
        out_shape=jax.ShapeDtypeStruct((t, MIX_W), F32),
        grid=(batch, nt),
        in_specs=[col(3), col(4), col(1),
                  pl.BlockSpec((ts, GATE_COLS), lambda b, s: (b * nt + s, 0))],
        out_specs=pl.BlockSpec((ts, MIX_W), lambda b, s: (b * nt + s, 0)),
        scratch_shapes=[pltpu.VMEM((MIX_HEADS, HEAD_DIM, 2 * HEAD_DIM), F32),
                        pltpu.VMEM((MIX_HEADS, 8, 128), F32)],
        compiler_params=pltpu.CompilerParams(dimension_semantics=("parallel", "arbitrary"),
                                             vmem_limit_bytes=VMEM_LIMIT),
        name="mlstm",
    )(qkv, qkv, zvo, gates)


def _mix_xattn_kernel(x_ref, go_ref, mh_ref, z_ref, mo_ref, gnw_ref, mnw_ref, wout_ref,
                      nxa_ref, wq_ref, k_ref, v_ref, wo_ref, out_ref):
    parts = []
    gnw = gnw_ref[...]
    for h in range(MIX_HEADS):
        cols = slice(h * HEAD_DIM, (h + 1) * HEAD_DIM)
        z = z_ref[:, cols]
        parts.append(_rms(go_ref[:, cols], gnw) * (z * _sigmoid(z)))
    for h in range(MIX_HEADS):
        cols = slice(h * HEAD_DIM, (h + 1) * HEAD_DIM)
        parts.append(_rms(mh_ref[:, cols], mnw_ref[:, cols]) * _sigmoid(mo_ref[:, cols]))
    mixed = jnp.concatenate(parts, axis=1).astype(BF16)
    x1 = x_ref[...] + _dot(mixed, wout_ref[...])

    xn = _rms(x1, nxa_ref[...]).astype(BF16)
    q = _dot(xn, wq_ref[...])
    kk = k_ref[0]
    vv = v_ref[0]
    outs = []
    for h in range(XA_HEADS):
        cols = slice(h * XA_HEAD_DIM, (h + 1) * XA_HEAD_DIM)
        s = _dot_nt(q[:, cols].astype(BF16), kk[:, cols]) * (XA_HEAD_DIM ** -0.5)
        s = s - jnp.max(s, axis=-1, keepdims=True)
        e = jnp.exp(s)
        p = e / jnp.sum(e, axis=-1, keepdims=True)
        outs.append(_dot(p.astype(BF16), vv[:, cols]))
    o = jnp.concatenate(outs, axis=1).astype(BF16)
    out_ref[...] = x1 + _dot(o, wo_ref[...])


def _mix_xattn(x2d, go, mh, zvo, gnw, mnw, wout, nxa, wq, kmem, vmem, wo, *, seq, tm):
    t = x2d.shape[0]
    per_seq = seq // tm

    def row(width, j=0):
        return pl.BlockSpec((tm, width), lambda i, j=j: (i, j))

    def const(shape):
        return pl.BlockSpec(shape, lambda i: (0,) * len(shape))

    mem_spec = pl.BlockSpec((1, MEM_TOKENS, D_MODEL), lambda i: (i // per_seq, 0, 0))
    return pl.pallas_call(
        _mix_xattn_kernel,
        out_shape=jax.ShapeDtypeStruct((t, D_MODEL), F32),
        grid=(t // tm,),
        in_specs=[row(D_MODEL), row(MIX_W), row(MIX_W), row(MIX_W, 0), row(MIX_W, 2),
                  const((1, HEAD_DIM)), const((1, MIX_W)), const((D_MODEL, D_MODEL)),
                  const((1, D_MODEL)), const((D_MODEL, D_MODEL)), mem_spec, mem_spec,
                  const((D_MODEL, D_MODEL))],
        out_specs=row(D_MODEL),
        compiler_params=pltpu.CompilerParams(dimension_semantics=("parallel",),
                                             vmem_limit_bytes=VMEM_LIMIT),
        name="mix_xattn",
    )(x2d, go, mh, zvo, zvo, gnw, mnw, wout, nxa, wq, kmem, vmem, wo)


def _take_top16(values, order, exact, want_steps):
    tt = values.shape[1]
    slot = lax.broadcasted_iota(jnp.int32, (PEER_TOPK, tt), 0)
    v = values
    step = jnp.full(values.shape, float(PEER_TOPK), F32) if want_steps else None
    vals = jnp.zeros((PEER_TOPK, tt), F32)
    for kk in range(PEER_TOPK):
        m = jnp.max(v, axis=0, keepdims=True)
        if exact:
            first = jnp.min(jnp.where(v == m, order, 1e9), axis=0, keepdims=True)
            sel = order == first
        else:
            sel = v == m
        if want_steps:
            step = jnp.where(sel, float(kk), step)
        v = jnp.where(sel, -jnp.inf, v)
        vals = jnp.where(slot == kk, m, vals)
    taken = v < values
    count = jnp.sum(jnp.where(taken, 1.0, 0.0), axis=0, keepdims=True)
    return taken, step, vals, count


def _candidate_pairs(a, b):
    tt = a.shape[1]
    sub = lax.broadcasted_iota(jnp.int32, (8, 1), 0)
    zero = jnp.zeros((8, 1), jnp.int32)

    def row(x, p):
        return jnp.broadcast_to(x[p:p + 1], (8, tt))

    b_lo4 = jnp.where(sub < 4, b[0:8], pltpu.roll(b[0:8], 4, 0))
    b_01 = jnp.where(sub % 2 == 0, row(b, 0), row(b, 1))
    groups = (
        (row(a, 0), b[0:8], zero, sub, sub < 8),
        (row(a, 0), b[8:16], zero, sub + 8, sub < 8),
        (row(a, 1), b[0:8], zero + 1, sub, sub < 8),
        (row(a, 2), b[0:8], zero + 2, sub, sub < 5),
        (jnp.where(sub < 4, row(a, 3), row(a, 4)), b_lo4,
         jnp.where(sub < 4, 3, 4), sub % 4, sub < 7),
        (jnp.where(sub < 2, row(a, 5), jnp.where(sub < 4, row(a, 6), row(a, 7))), b_01,
         5 + sub // 2, sub % 2, sub < 6),
        (a[8:16], row(b, 0), sub + 8, zero, sub < 8),
    )
    cand = jnp.concatenate([g[0] + g[1] for g in groups], axis=0)
    pidx = jnp.concatenate([g[2] for g in groups], axis=0)
    qidx = jnp.concatenate([g[3] for g in groups], axis=0)
    valid = jnp.concatenate([g[4] for g in groups], axis=0)
    return cand, pidx, qidx, valid


def _route_kernel(x_ref, nw_ref, wq_ref, keys_ref, xnt_ref, c0_ref, ap_ref, r1_ref, bp_ref,
                  xn_scr, *, tt):
    @pl.when(pl.program_id(1) == 0)
    def _():
        xn32 = _rms(x_ref[...], nw_ref[...])
        xn_scr[...] = xn32.astype(BF16)
        xnt_ref[...] = xn32.T.astype(BF16)

    xn = xn_scr[...]
    q = _dot(xn, wq_ref[...])
    s0 = _dot_nt(keys_ref[0, 0], q[:, :PEER_HALF].astype(BF16))
    s1 = _dot_nt(keys_ref[0, 1], q[:, PEER_HALF:].astype(BF16))

    def tables(exact):
        kidx = lax.broadcasted_iota(jnp.int32, s0.shape, 0).astype(F32)
        _, r0, a, n0 = _take_top16(s0, kidx, exact, want_steps=exact)
        _, r1, b, n1 = _take_top16(s1, kidx, exact, want_steps=True)

        cand, pidx, qidx, valid = _candidate_pairs(a, b)
        pos = jnp.broadcast_to((pidx * PEER_TOPK + qidx).astype(F32), cand.shape)
        taken, _, _, n2 = _take_top16(jnp.where(valid, cand, -jnp.inf), pos, exact, want_steps=False)
        chosen = jnp.where(jnp.logical_and(valid, taken), 1.0, 0.0)

        top = a[0:1] + b[0:1]
        gsel = chosen * jnp.exp(jnp.where(valid, cand, top) - top)
        zinv = 1.0 / jnp.sum(gsel, axis=0, keepdims=True)

        c0 = jnp.zeros(s0.shape, F32)
        for p in range(PEER_TOPK):
            cnt = jnp.sum(jnp.where(pidx == p, chosen, 0.0), axis=0, keepdims=True)
            has_rank_p = (r0 == float(p)) if exact else (s0 == a[p:p + 1])
            c0 = jnp.where(has_rank_p, cnt, c0)

        c0_ref[0] = c0
        ap_ref[0] = jnp.exp(s0 - a[0:1]) * (zinv * math.sqrt(0.5))
        r1_ref[0] = (r1 * RANK_SCALE).astype(BF16)
        bp_ref[0] = jnp.exp(s1 - b[0:1]).astype(BF16)
        return n0 + n1 + n2

    taken_total = tables(exact=False)
    tied = jnp.max(jnp.abs(taken_total - 3.0 * PEER_TOPK)) > 0.0

    @pl.when(tied)
    def _():
        tables(exact=True)


def _peer_route(x2, nw, wq, keys, *, tt):
    t = x2.shape[0]
    kern = functools.partial(_route_kernel, tt=tt)
    tab = lambda dt: jax.ShapeDtypeStruct((PEER_HEADS, PEER_KEYS, t), dt)
    tab_spec = pl.BlockSpec((1, PEER_KEYS, tt), lambda i, h: (h, 0, i))
    return pl.pallas_call(
        kern,
        out_shape=(jax.ShapeDtypeStruct((D_MODEL, t), BF16), tab(F32), tab(F32), tab(BF16), tab(BF16)),
        grid=(t // tt, PEER_HEADS),
        in_specs=[pl.BlockSpec((tt, D_MODEL), lambda i, h: (i, 0)),
                  pl.BlockSpec((1, D_MODEL), lambda i, h: (0, 0)),
                  pl.BlockSpec((D_MODEL, 2 * PEER_HALF), lambda i, h: (0, h)),
                  pl.BlockSpec((1, 2, PEER_KEYS, PEER_HALF), lambda i, h: (h, 0, 0, 0))],
        out_specs=(pl.BlockSpec((D_MODEL, tt), lambda i, h: (0, i)),
                   tab_spec, tab_spec, tab_spec, tab_spec),
        scratch_shapes=[pltpu.VMEM((tt, D_MODEL), BF16)],
        compiler_params=pltpu.CompilerParams(dimension_semantics=("parallel", "arbitrary"),
                                             vmem_limit_bytes=VMEM_LIMIT),
        name="peer_route",
    )(x2, nw, wq, keys)


def _peer_eval_kernel(xnt_ref, x_ref, u_ref, v_ref, c0_ref, ap_ref, r1_ref, bp_ref, fnw_ref,
                      out_ref, acc_ref, w_scr, *, eb, sb):
    e = pl.program_id(1)

    @pl.when(e == 0)
    def _():
        acc_ref[...] = jnp.zeros_like(acc_ref)

    tt = xnt_ref.shape[1]
    nsub = eb // sb
    ipb = sb // PEER_KEYS

    def activations(s):
        return _dot(u_ref[s * sb:(s + 1) * sb, :], xnt_ref[...])

    def row_tile(ref, h, il, lanes, scale):
        row = ref[h, il:il + 1, lanes] * scale
        return jnp.broadcast_to(row, (BF16_ROWS, LANES)).astype(BF16)[None]

    def weight_slab(s, j):
        ii, lg = divmod(j, tt // LANES)
        il = s * ipb + ii
        lanes = slice(lg * LANES, (lg + 1) * LANES)
        w = None
        for h in range(PEER_HEADS):
            gate = jnp.clip(row_tile(c0_ref, h, il, lanes, RANK_SCALE) - r1_ref[h, :, :, lanes],
                            0.0, row_tile(ap_ref, h, il, lanes, 1.0))
            term = gate * bp_ref[h, :, :, lanes]
            w = term if w is None else w + term
        w_scr[ii * PEER_KEYS:(ii + 1) * PEER_KEYS, lanes] = w.reshape(PEER_KEYS, LANES)

    act = activations(0)
    part = None
    for s in range(nsub):
        act_next = activations(s + 1) if s + 1 < nsub else None
        for j in range(ipb * (tt // LANES)):
            weight_slab(s, j)
        z = act.astype(BF16)
        pt = w_scr[...] * (z * (1.0 + lax.erf(z)))
        d = _dot_tn(pt, v_ref[s * sb:(s + 1) * sb, :])
        part = d if part is None else part + d
        act = act_next
    acc_ref[...] += part

    @pl.when(e == pl.num_programs(1) - 1)
    def _():
        out_ref[...] = _rms(x_ref[...] + acc_ref[...], fnw_ref[...])


def _peer_eval(xn, x2, u16, v16, c0, ap, r1, bp, fnw, *, tt, eb, sb):
    t = x2.shape[0]
    ne = u16.shape[0]
    ib = eb // PEER_KEYS
    kern = functools.partial(_peer_eval_kernel, eb=eb, sb=sb)
    half0 = pl.BlockSpec((PEER_HEADS, ib, tt), lambda i, e: (0, e, i))
    ktiles = PEER_KEYS // BF16_ROWS
    r1 = r1.reshape(PEER_HEADS, ktiles, BF16_ROWS, t)
    bp = bp.reshape(PEER_HEADS, ktiles, BF16_ROWS, t)
    half1 = pl.BlockSpec((PEER_HEADS, ktiles, BF16_ROWS, tt), lambda i, e: (0, 0, 0, i))
    return pl.pallas_call(
        kern,
        out_shape=jax.ShapeDtypeStruct((t, D_MODEL), F32),
        grid=(t // tt, ne // eb),
        in_specs=[pl.BlockSpec((D_MODEL, tt), lambda i, e: (0, i)),
                  pl.BlockSpec((tt, D_MODEL), lambda i, e: (i, 0)),
                  pl.BlockSpec((eb, D_MODEL), lambda i, e: (e, 0)),
                  pl.BlockSpec((eb, D_MODEL), lambda i, e: (e, 0)),
                  half0, half0, half1, half1,
                  pl.BlockSpec((1, D_MODEL), lambda i, e: (0, 0))],
        out_specs=pl.BlockSpec((tt, D_MODEL), lambda i, e: (i, 0)),
        scratch_shapes=[pltpu.VMEM((tt, D_MODEL), F32), pltpu.VMEM((sb, tt), BF16)],
        compiler_params=pltpu.CompilerParams(dimension_semantics=("parallel", "arbitrary"),
                                             vmem_limit_bytes=VMEM_LIMIT),
        name="peer_eval",
    )(xn, x2, u16, v16, c0, ap, r1, bp, fnw)


def kernel(x, mem, norm_mix_w, w_in, gdn_conv_w, gdn_a_log, gdn_dt_bias, gdn_norm_w, mlstm_conv_w,
           mlstm_i_bias, mlstm_f_bias, mlstm_norm_w, w_out, norm_xa_w, norm_mem_w, xa_wq, xa_wkv,
           xa_wo, norm_ffn_w, peer_wq, peer_sub_keys, peer_u, peer_v, norm_final_w):
    batch, seq, d = x.shape
    t = batch * seq
    assert d == D_MODEL and w_in.shape[0] == 1, "single-layer block; the final norm is fused into it"
    tm = min(256, seq)
    ts = min(256, seq)
    tt = min(512, t)
    xs = x.reshape(t, d)
    mem2d = mem.reshape(batch * MEM_TOKENS, d)

    wl = w_in[0]
    o_gz = 3 * MIX_W
    o_ga = o_gz + MIX_W
    o_gb = o_ga + MIX_HEADS
    o_mqk = o_gb + MIX_HEADS
    o_mv = o_mqk + 2 * MIX_W
    o_mo = o_mv + MIX_W
    o_mi = o_mo + MIX_W
    o_mf = o_mi + MIX_HEADS
    w_all = jnp.concatenate([wl[:, 0:o_gz], wl[:, o_mqk:o_mv], wl[:, o_gz:o_ga],
                             wl[:, o_mv:o_mo], wl[:, o_mo:o_mi],
                             wl[:, o_ga:o_mqk], wl[:, o_mi:o_mf + MIX_HEADS],
                             jnp.zeros((d, GATE_COLS - 4 * MIX_HEADS), wl.dtype)], axis=1).astype(BF16)
    cw = jnp.concatenate([gdn_conv_w[0], mlstm_conv_w[0]], axis=1).astype(F32)
    zeros4 = jnp.zeros((MIX_HEADS,), F32)
    pad = jnp.zeros((GATE_COLS - 4 * MIX_HEADS,), F32)
    gbias = jnp.concatenate([gdn_dt_bias[0], zeros4, mlstm_i_bias[0], mlstm_f_bias[0], pad]
                            ).astype(F32).reshape(1, GATE_COLS)
    alog = jnp.concatenate([gdn_a_log[0], zeros4, zeros4, zeros4, pad]).astype(F32).reshape(1, GATE_COLS)

    kmem, vmem = _memkv(mem2d, norm_mem_w[0].reshape(1, d), xa_wkv[0].astype(BF16))
    qkv, zvo, gates = _inproj(xs, norm_mix_w[0].reshape(1, d), w_all, cw, gbias, alog,
                              seq=seq, tm=tm)
    go = _gdn(qkv, gates, batch=batch, seq=seq, ts=ts)
    mh = _mlstm(qkv, zvo, gates, batch=batch, seq=seq, ts=ts)
    x2 = _mix_xattn(xs, go, mh, zvo, gdn_norm_w[0].reshape(1, HEAD_DIM),
                    mlstm_norm_w[0].reshape(1, MIX_W), w_out[0].astype(BF16),
                    norm_xa_w[0].reshape(1, d), xa_wq[0].astype(BF16),
                    kmem.reshape(batch, MEM_TOKENS, d), vmem.reshape(batch, MEM_TOKENS, d),
                    xa_wo[0].astype(BF16), seq=seq, tm=tm)
    xn, c0, ap, r1, bp = _peer_route(x2, norm_ffn_w[0].reshape(1, d), peer_wq[0].astype(BF16),
                                     peer_sub_keys[0].astype(BF16), tt=tt)
    u16 = (peer_u[0] * math.sqrt(0.5)).astype(BF16)
    out = _peer_eval(xn, x2, u16, peer_v[0].astype(BF16), c0, ap, r1, bp,
                     norm_final_w.reshape(1, d), tt=tt, eb=4096, sb=256)
    return out.reshape(batch, seq, d)
```

```python
import functools
import math

import jax
import jax.numpy as jnp
from jax import lax
from jax.experimental import pallas as pl
from jax.experimental.pallas import tpu as pltpu

F32 = jnp.float32
BF16 = jnp.bfloat16
HIGHEST = lax.Precision.HIGHEST

D_MODEL = 1024
HEAD_DIM = 128
MIX_HEADS = 4
MIX_W = MIX_HEADS * HEAD_DIM
CONV_WIDTH = 4
CHUNK = 64
MEM_TOKENS = 256
XA_HEADS = 4
XA_HEAD_DIM = D_MODEL // XA_HEADS
PEER_HEADS = 8
PEER_KEYS = 128
PEER_TOPK = 16
PEER_HALF = 128
NORM_EPS = 1e-6
NEG_BIG = -1e30
LANES = 128
MXU_COLS = 256
BF16_ROWS = 16
RANK_SCALE = 2.0

CONV_COLS = 3 * MIX_W + 2 * MIX_W
PLAIN_COLS = 3 * MIX_W
GATE_COLS = 128
PROJ_COLS = CONV_COLS + PLAIN_COLS + GATE_COLS
HALO = 8

VMEM_LIMIT = 60 * 1024 * 1024

NT_DIMS = (((1,), (1,)), ((), ()))
TN_DIMS = (((0,), (0,)), ((), ()))


def _rms(x, w):
    return x * lax.rsqrt(jnp.mean(x * x, axis=-1, keepdims=True) + NORM_EPS) * w


def _sigmoid(x):
    return 1.0 / (1.0 + jnp.exp(-x))


def _softplus(x):
    return jnp.maximum(x, 0.0) + jnp.log1p(jnp.exp(-jnp.abs(x)))


def _dot(a, b):
    return jnp.dot(a, b, preferred_element_type=F32)


def _dot_nt(a, b, precision=None):
    return lax.dot_general(a, b, NT_DIMS, precision=precision, preferred_element_type=F32)


def _dot_tn(a, b):
    return lax.dot_general(a, b, TN_DIMS, preferred_element_type=F32)


def _memkv_kernel(m_ref, nw_ref, w_ref, k_ref, v_ref):
    mn = _rms(m_ref[...], nw_ref[...]).astype(BF16)
    kv = _dot(mn, w_ref[...])
    k_ref[...] = kv[:, :D_MODEL].astype(BF16)
    v_ref[...] = kv[:, D_MODEL:].astype(BF16)


def _memkv(mem2d, nw, wkv):
    n = mem2d.shape[0]
    tm = MEM_TOKENS
    return pl.pallas_call(
        _memkv_kernel,
        out_shape=(jax.ShapeDtypeStruct((n, D_MODEL), BF16),
                   jax.ShapeDtypeStruct((n, D_MODEL), BF16)),
        grid=(n // tm,),
        in_specs=[pl.BlockSpec((tm, D_MODEL), lambda i: (i, 0)),
                  pl.BlockSpec((1, D_MODEL), lambda i: (0, 0)),
                  pl.BlockSpec((D_MODEL, 2 * D_MODEL), lambda i: (0, 0))],
        out_specs=(pl.BlockSpec((tm, D_MODEL), lambda i: (i, 0)),
                   pl.BlockSpec((tm, D_MODEL), lambda i: (i, 0))),
        compiler_params=pltpu.CompilerParams(dimension_semantics=("parallel",),
                                             vmem_limit_bytes=VMEM_LIMIT),
        name="memkv",
    )(mem2d, nw, wkv)


def _inproj_kernel(x_ref, halo_ref, nw_ref, w_ref, cw_ref, gbias_ref, alog_ref,
                   qkv_ref, zvo_ref, gate_ref, p_scr, *, tm, seq):
    i = pl.program_id(0)
    xc = jnp.concatenate([halo_ref[...], x_ref[...]], axis=0)
    xn = _rms(xc, nw_ref[...]).astype(BF16)
    p = _dot(xn, w_ref[...])

    first_row = jnp.where((i * tm) % seq == 0, HALO, 0)
    row = lax.broadcasted_iota(jnp.int32, (HALO, 1), 0)
    p_scr[:HALO, :] = jnp.where(row >= first_row, p[:HALO, :CONV_COLS], 0.0)
    p_scr[HALO:, :] = p[HALO:, :CONV_COLS]
    cw = cw_ref[...]
    y = cw[CONV_WIDTH - 1:CONV_WIDTH] * p_scr[HALO:HALO + tm, :]
    for kk in range(CONV_WIDTH - 1):
        off = HALO - (CONV_WIDTH - 1) + kk
        y = y + cw[kk:kk + 1] * p_scr[off:off + tm, :]
    y = y * _sigmoid(y)

    scale = HEAD_DIM ** -0.5
    for hh in range(2 * MIX_HEADS):
        cols = slice(hh * HEAD_DIM, (hh + 1) * HEAD_DIM)
        yh = y[:, cols]
        inv = lax.rsqrt(jnp.sum(yh * yh, axis=-1, keepdims=True) + NORM_EPS)
        if hh < MIX_HEADS:
            inv = inv * scale
        qkv_ref[:, cols] = yh * inv
    qkv_ref[:, 2 * MIX_W:4 * MIX_W] = y[:, 2 * MIX_W:4 * MIX_W]
    qkv_ref[:, 4 * MIX_W:5 * MIX_W] = y[:, 4 * MIX_W:5 * MIX_W] * scale

    zvo_ref[...] = p[HALO:, CONV_COLS:CONV_COLS + PLAIN_COLS]

    ga = p[HALO:, CONV_COLS + PLAIN_COLS:] + gbias_ref[...]
    lane = lax.broadcasted_iota(jnp.int32, (tm, GATE_COLS), 1)
    log_alpha = -jnp.exp(alog_ref[...]) * _softplus(ga)
    beta = _sigmoid(ga)
    log_f = -_softplus(-ga)
    gate_ref[...] = jnp.where(lane < 4, log_alpha,
                              jnp.where(lane < 8, beta,
                                        jnp.where(lane < 12, ga,
                                                  jnp.where(lane < 16, log_f, 0.0))))


def _inproj(x2d, nw, w_all, cw, gbias, alog, *, seq, tm):
    t = x2d.shape[0]
    hb = tm // HALO
    kern = functools.partial(_inproj_kernel, tm=tm, seq=seq)
    return pl.pallas_call(
        kern,
        out_shape=(jax.ShapeDtypeStruct((t, CONV_COLS), F32),
                   jax.ShapeDtypeStruct((t, PLAIN_COLS), F32),
                   jax.ShapeDtypeStruct((t, GATE_COLS), F32)),
        grid=(t // tm,),
        in_specs=[pl.BlockSpec((tm, D_MODEL), lambda i: (i, 0)),
                  pl.BlockSpec((HALO, D_MODEL), lambda i: (jnp.maximum(i * hb - 1, 0), 0)),
                  pl.BlockSpec((1, D_MODEL), lambda i: (0, 0)),
                  pl.BlockSpec((D_MODEL, PROJ_COLS), lambda i: (0, 0)),
                  pl.BlockSpec((CONV_WIDTH, CONV_COLS), lambda i: (0, 0)),
                  pl.BlockSpec((1, GATE_COLS), lambda i: (0, 0)),
                  pl.BlockSpec((1, GATE_COLS), lambda i: (0, 0))],
        out_specs=(pl.BlockSpec((tm, CONV_COLS), lambda i: (i, 0)),
                   pl.BlockSpec((tm, PLAIN_COLS), lambda i: (i, 0)),
                   pl.BlockSpec((tm, GATE_COLS), lambda i: (i, 0))),
        scratch_shapes=[pltpu.VMEM((HALO + tm, CONV_COLS), F32)],
        compiler_params=pltpu.CompilerParams(dimension_semantics=("parallel",),
                                             vmem_limit_bytes=VMEM_LIMIT),
        name="inproj",
    )(x2d, x2d, nw, w_all, cw, gbias, alog)


def _chunk_cumsum(g, ts):
    ri = lax.broadcasted_iota(jnp.int32, (ts, ts), 0)
    ci = lax.broadcasted_iota(jnp.int32, (ts, ts), 1)
    shift = CHUNK.bit_length() - 1
    same_chunk = jnp.right_shift(ri, shift) == jnp.right_shift(ci, shift)
    tri = jnp.where(jnp.logical_and(same_chunk, ci <= ri), 1.0, 0.0)
    return jnp.dot(tri, g, precision=HIGHEST, preferred_element_type=F32)


def _lane_pick(x, lane, idx):
    return jnp.where(lane == idx, x, 0.0)


def _gdn_kernel(q_ref, k_ref, v_ref, g_ref, o_ref, s_ref, *, ts):
    @pl.when(pl.program_id(1) == 0)
    def _():
        s_ref[...] = jnp.zeros_like(s_ref)

    L = CHUNK
    nch = ts // L
    g = g_ref[...]
    cs = _chunk_cumsum(g, ts)
    lane = lax.broadcasted_iota(jnp.int32, (L, GATE_COLS), 1)
    r = lax.broadcasted_iota(jnp.int32, (L, L), 0)
    c = lax.broadcasted_iota(jnp.int32, (L, L), 1)
    causal = r >= c
    strict = r > c
    eye = jnp.where(r == c, 1.0, 0.0)

    items = [(ch, h) for ch in range(nch) for h in range(MIX_HEADS)]
    n = len(items)

    def rows_of(ch):
        return slice(ch * L, (ch + 1) * L)

    def cols_of(h):
        return slice(h * HEAD_DIM, (h + 1) * HEAD_DIM)

    q = [q_ref[rows_of(ch), cols_of(h)] for ch, h in items]
    k = [k_ref[rows_of(ch), cols_of(h)] for ch, h in items]
    v = [v_ref[rows_of(ch), cols_of(h)] for ch, h in items]
    gc = [cs[rows_of(ch), h:h + 1] for ch, h in items]
    beta = [g[rows_of(ch), 4 + h:5 + h] for ch, h in items]
    gl = [cs[(ch + 1) * L - 1:(ch + 1) * L, h:h + 1] for ch, h in items]

    diff = []
    for ch, h in items:
        e_h = jnp.where(lane == h, 1.0, 0.0)
        cs_h = _lane_pick(cs[rows_of(ch)], lane, h)
        diff.append(_dot_nt(jnp.concatenate([cs_h, e_h], axis=1),
                            jnp.concatenate([e_h, -cs_h], axis=1), precision=HIGHEST))
    decay = [jnp.exp(jnp.where(causal, d, NEG_BIG)) for d in diff]
    eg = [jnp.exp(x) for x in gc]
    kb = [k[i] * beta[i] for i in range(n)]
    k16 = [x.astype(BF16) for x in k]
    q16 = [x.astype(BF16) for x in q]
    kk = [_dot_nt(kb[i].astype(BF16), k16[i]) for i in range(n)]
    qk = [_dot_nt(q16[i], k16[i]) for i in range(n)]
    a = [jnp.where(strict, kk[i] * decay[i], 0.0) for i in range(n)]
    qk16 = [(qk[i] * decay[i]).astype(BF16) for i in range(n)]
    tinv = [eye - x for x in a]
    pw = a
    for _ in range(5):
        pw16 = [x.astype(BF16) for x in pw]
        pw = [_dot(x, x) for x in pw16]
        tinv = [tinv[i] + _dot(tinv[i].astype(BF16), pw[i].astype(BF16)) for i in range(n)]
    rhs = [jnp.concatenate([v[i] * beta[i], kb[i] * eg[i]], axis=1).astype(BF16) for i in range(n)]
    uw = [_dot(tinv[i].astype(BF16), rhs[i]) for i in range(n)]
    wq16 = [jnp.concatenate([uw[i][:, HEAD_DIM:], q[i] * eg[i]], axis=0).astype(BF16)
            for i in range(n)]
    kd16 = [(k[i] * jnp.exp(gl[i] - gc[i])).astype(BF16) for i in range(n)]
    egl = [jnp.exp(x) for x in gl]

    state = [s_ref[h] for h in range(MIX_HEADS)]
    for ch in range(nch):
        base = ch * MIX_HEADS
        rr = [_dot(wq16[base + h], state[h].astype(BF16)) for h in range(MIX_HEADS)]
        vnew16 = [(uw[base + h][:, :HEAD_DIM] - rr[h][:L]).astype(BF16) for h in range(MIX_HEADS)]
        for h in range(MIX_HEADS):
            o_ref[rows_of(ch), cols_of(h)] = rr[h][L:] + _dot(qk16[base + h], vnew16[h])
        state = [state[h] * egl[base + h] + _dot_tn(kd16[base + h], vnew16[h])
                 for h in range(MIX_HEADS)]
    for h in range(MIX_HEADS):
        s_ref[h] = state[h]


def _gdn(qkv, gates, *, batch, seq, ts):
    t = qkv.shape[0]
    nt = seq // ts
    kern = functools.partial(_gdn_kernel, ts=ts)

    def col(j):
        return pl.BlockSpec((ts, MIX_W), lambda b, s, j=j: (b * nt + s, j))

    return pl.pallas_call(
        kern,
        out_shape=jax.ShapeDtypeStruct((t, MIX_W), F32),
        grid=(batch, nt),
        in_specs=[col(0), col(1), col(2),
                  pl.BlockSpec((ts, GATE_COLS), lambda b, s: (b * nt + s, 0))],
        out_specs=pl.BlockSpec((ts, MIX_W), lambda b, s: (b * nt + s, 0)),
        scratch_shapes=[pltpu.VMEM((MIX_HEADS, HEAD_DIM, HEAD_DIM), F32)],
        compiler_params=pltpu.CompilerParams(dimension_semantics=("parallel", "arbitrary"),
                                             vmem_limit_bytes=VMEM_LIMIT),
        name="gdn",
    )(qkv, qkv, qkv, gates)


def _mlstm_kernel(q_ref, k_ref, v_ref, g_ref, o_ref, c_ref, m_ref, *, ts):
    @pl.when(pl.program_id(1) == 0)
    def _():
        c_ref[...] = jnp.zeros_like(c_ref)
        m_ref[...] = jnp.zeros_like(m_ref)

    L = CHUNK
    nch = ts // L
    g = g_ref[...]
    cs = _chunk_cumsum(g, ts)
    lane = lax.broadcasted_iota(jnp.int32, (L, GATE_COLS), 1)
    r = lax.broadcasted_iota(jnp.int32, (L, L), 0)
    c = lax.broadcasted_iota(jnp.int32, (L, L), 1)
    causal = r >= c
    ones = jnp.ones((L, HEAD_DIM), F32)

    items = [(ch, h) for ch in range(nch) for h in range(MIX_HEADS)]
    n = len(items)

    def rows_of(ch):
        return slice(ch * L, (ch + 1) * L)

    def cols_of(h):
        return slice(h * HEAD_DIM, (h + 1) * HEAD_DIM)

    q16 = [q_ref[rows_of(ch), cols_of(h)].astype(BF16) for ch, h in items]
    k = [k_ref[rows_of(ch), cols_of(h)] for ch, h in items]
    vaug = [jnp.concatenate([v_ref[rows_of(ch), cols_of(h)], ones], axis=1).astype(BF16)
            for ch, h in items]
    bc = [cs[rows_of(ch), 12 + h:13 + h] for ch, h in items]
    li = [g[rows_of(ch), 8 + h:9 + h] for ch, h in items]
    bl = [cs[(ch + 1) * L - 1:(ch + 1) * L, 12 + h:13 + h] for ch, h in items]
    dmat = []
    for ch, h in items:
        csch = cs[rows_of(ch)]
        gch = g[rows_of(ch)]
        e_f = jnp.where(lane == 12 + h, 1.0, 0.0)
        e_if = jnp.where(jnp.logical_or(lane == 8 + h, lane == 12 + h), 1.0, 0.0)
        xm = jnp.concatenate([_lane_pick(csch, lane, 12 + h), e_if], axis=1)
        ym = jnp.concatenate([e_f, _lane_pick(gch, lane, 8 + h) - _lane_pick(csch, lane, 12 + h)],
                             axis=1)
        dmat.append(jnp.where(causal, _dot_nt(xm, ym, precision=HIGHEST), NEG_BIG))
    qk = [_dot_nt(q16[i], k[i].astype(BF16)) for i in range(n)]
    rowmax = [jnp.max(d, axis=1, keepdims=True) for d in dmat]
    dl = [bl[i] - bc[i] + li[i] for i in range(n)]
    dlmax = [jnp.max(x, axis=0, keepdims=True) for x in dl]

    m_in = []
    m_cur = [m_ref[h][0:1, 0:1] for h in range(MIX_HEADS)]
    for i, (ch, h) in enumerate(items):
        m_in.append(m_cur[h])
        m_cur[h] = jnp.maximum(bl[i] + m_cur[h], dlmax[i])
    m_out = [m_cur[h] if ch == nch - 1 else m_in[(ch + 1) * MIX_HEADS + h] for ch, h in items]
    for h in range(MIX_HEADS):
        m_ref[h] = jnp.broadcast_to(m_cur[h], m_ref.shape[1:])

    inter = [bc[i] + m_in[i] for i in range(n)]
    mt = [jnp.maximum(inter[i], rowmax[i]) for i in range(n)]
    p16 = [(qk[i] * jnp.exp(dmat[i] - mt[i])).astype(BF16) for i in range(n)]
    sc = [jnp.exp(inter[i] - mt[i]) for i in range(n)]
    pv = [_dot(p16[i], vaug[i]) for i in range(n)]
    kw16 = [(k[i] * jnp.exp(dl[i] - m_out[i])).astype(BF16) for i in range(n)]
    kv = [_dot_tn(kw16[i], vaug[i]) for i in range(n)]
    dec = [jnp.exp(bl[i] + m_in[i] - m_out[i]) for i in range(n)]

    cst = [c_ref[h] for h in range(MIX_HEADS)]
    c_in = []
    for i, (ch, h) in enumerate(items):
        c_in.append(cst[h].astype(BF16))
        cst[h] = dec[i] * cst[h] + kv[i]
    for h in range(MIX_HEADS):
        c_ref[h] = cst[h]

    for i, (ch, h) in enumerate(items):
        nd = sc[i] * _dot(q16[i], c_in[i]) + pv[i]
        den = jnp.maximum(jnp.abs(nd[:, HEAD_DIM:]), jnp.exp(-mt[i]))
        o_ref[rows_of(ch), cols_of(h)] = nd[:, :HEAD_DIM] / den


def _mlstm(qkv, zvo, gates, *, batch, seq, ts):
    t = qkv.shape[0]
    nt = seq // ts
    kern = functools.partial(_mlstm_kernel, ts=ts)

    def col(j):
        return pl.BlockSpec((ts, MIX_W), lambda b, s, j=j: (b * nt + s, j))

    return pl.pallas_call(
        kern,
        out_shape=jax.ShapeDtypeStruct((t, MIX_W), F32),
        grid=(batch, nt),
        in_specs=[col(3), col(4), col(1),
                  pl.BlockSpec((ts, GATE_COLS), lambda b, s: (b * nt + s, 0))],
        out_specs=pl.BlockSpec((ts, MIX_W), lambda b, s: (b * nt + s, 0)),
        scratch_shapes=[pltpu.VMEM((MIX_HEADS, HEAD_DIM, 2 * HEAD_DIM), F32),
                        pltpu.VMEM((MIX_HEADS, 8, 128), F32)],
        compiler_params=pltpu.CompilerParams(dimension_semantics=("parallel", "arbitrary"),
                                             vmem_limit_bytes=VMEM_LIMIT),
        name="mlstm",
    )(qkv, qkv, zvo, gates)


def _mix_xattn_kernel(x_ref, go_ref, mh_ref, z_ref, mo_ref, gnw_ref, mnw_ref, wout_ref,
                      nxa_ref, wq_ref, k_ref, v_ref, wo_ref, out_ref):
    parts = []
    gnw = gnw_ref[...]
    for h in range(MIX_HEADS):
        cols = slice(h * HEAD_DIM, (h + 1) * HEAD_DIM)
        z = z_ref[:, cols]
        parts.append(_rms(go_ref[:, cols], gnw) * (z * _sigmoid(z)))
    for h in range(MIX_HEADS):
        cols = slice(h * HEAD_DIM, (h + 1) * HEAD_DIM)
        parts.append(_rms(mh_ref[:, cols], mnw_ref[:, cols]) * _sigmoid(mo_ref[:, cols]))
    mixed = jnp.concatenate(parts, axis=1).astype(BF16)
    x1 = x_ref[...] + _dot(mixed, wout_ref[...])

    xn = _rms(x1, nxa_ref[...]).astype(BF16)
    q = _dot(xn, wq_ref[...])
    kk = k_ref[0]
    vv = v_ref[0]
    heads = [slice(h * XA_HEAD_DIM, (h + 1) * XA_HEAD_DIM) for h in range(XA_HEADS)]
    s = [_dot_nt(q[:, c].astype(BF16), kk[:, c]) * (XA_HEAD_DIM ** -0.5) for c in heads]
    e = [jnp.exp(x - jnp.max(x, axis=-1, keepdims=True)) for x in s]
    inv = [1.0 / jnp.sum(x, axis=-1, keepdims=True) for x in e]
    pv = [_dot(e[h].astype(BF16), vv[:, heads[h]]) for h in range(XA_HEADS)]
    o = jnp.concatenate([pv[h] * inv[h] for h in range(XA_HEADS)], axis=1).astype(BF16)
    out_ref[...] = x1 + _dot(o, wo_ref[...])


def _mix_xattn(x2d, go, mh, zvo, gnw, mnw, wout, nxa, wq, kmem, vmem, wo, *, seq, tm):
    t = x2d.shape[0]
    per_seq = seq // tm

    def row(width, j=0):
        return pl.BlockSpec((tm, width), lambda i, j=j: (i, j))

    def const(shape):
        return pl.BlockSpec(shape, lambda i: (0,) * len(shape))

    mem_spec = pl.BlockSpec((1, MEM_TOKENS, D_MODEL), lambda i: (i // per_seq, 0, 0))
    return pl.pallas_call(
        _mix_xattn_kernel,
        out_shape=jax.ShapeDtypeStruct((t, D_MODEL), F32),
        grid=(t // tm,),
        in_specs=[row(D_MODEL), row(MIX_W), row(MIX_W), row(MIX_W, 0), row(MIX_W, 2),
                  const((1, HEAD_DIM)), const((1, MIX_W)), const((D_MODEL, D_MODEL)),
                  const((1, D_MODEL)), const((D_MODEL, D_MODEL)), mem_spec, mem_spec,
                  const((D_MODEL, D_MODEL))],
        out_specs=row(D_MODEL),
        compiler_params=pltpu.CompilerParams(dimension_semantics=("parallel",),
                                             vmem_limit_bytes=VMEM_LIMIT),
        name="mix_xattn",
    )(x2d, go, mh, zvo, zvo, gnw, mnw, wout, nxa, wq, kmem, vmem, wo)


def _take_top16(values, order, exact, want_steps):
    tt = values.shape[1]
    slot = lax.broadcasted_iota(jnp.int32, (PEER_TOPK, tt), 0)
    v = values
    step = jnp.full(values.shape, float(PEER_TOPK), F32) if want_steps else None
    vals = jnp.zeros((PEER_TOPK, tt), F32)
    for kk in range(PEER_TOPK):
        m = jnp.max(v, axis=0, keepdims=True)
        if exact:
            first = jnp.min(jnp.where(v == m, order, 1e9), axis=0, keepdims=True)
            sel = order == first
        else:
            sel = v == m
        if want_steps:
            step = jnp.where(sel, float(kk), step)
        v = jnp.where(sel, -jnp.inf, v)
        vals = jnp.where(slot == kk, m, vals)
    taken = v < values
    count = jnp.sum(jnp.where(taken, 1.0, 0.0), axis=0, keepdims=True)
    return taken, step, vals, count


def _candidate_pairs(a, b):
    tt = a.shape[1]
    sub = lax.broadcasted_iota(jnp.int32, (8, 1), 0)
    zero = jnp.zeros((8, 1), jnp.int32)

    def row(x, p):
        return jnp.broadcast_to(x[p:p + 1], (8, tt))

    b_lo4 = jnp.where(sub < 4, b[0:8], pltpu.roll(b[0:8], 4, 0))
    b_01 = jnp.where(sub % 2 == 0, row(b, 0), row(b, 1))
    groups = (
        (row(a, 0), b[0:8], zero, sub, sub < 8),
        (row(a, 0), b[8:16], zero, sub + 8, sub < 8),
        (row(a, 1), b[0:8], zero + 1, sub, sub < 8),
        (row(a, 2), b[0:8], zero + 2, sub, sub < 5),
        (jnp.where(sub < 4, row(a, 3), row(a, 4)), b_lo4,
         jnp.where(sub < 4, 3, 4), sub % 4, sub < 7),
        (jnp.where(sub < 2, row(a, 5), jnp.where(sub < 4, row(a, 6), row(a, 7))), b_01,
         5 + sub // 2, sub % 2, sub < 6),
        (a[8:16], row(b, 0), sub + 8, zero, sub < 8),
    )
    cand = jnp.concatenate([g[0] + g[1] for g in groups], axis=0)
    pidx = jnp.concatenate([g[2] for g in groups], axis=0)
    qidx = jnp.concatenate([g[3] for g in groups], axis=0)
    valid = jnp.concatenate([g[4] for g in groups], axis=0)
    return cand, pidx, qidx, valid


def _route_kernel(x_ref, nw_ref, wq_ref, keys_ref, xnt_ref, c0_ref, ap_ref, r1_ref, bp_ref,
                  xn_scr, *, tt):
    @pl.when(pl.program_id(1) == 0)
    def _():
        xn32 = _rms(x_ref[...], nw_ref[...])
        xn_scr[...] = xn32.astype(BF16)
        xnt_ref[...] = xn32.T.astype(BF16)

    xn = xn_scr[...]
    q = _dot(xn, wq_ref[...])
    s0 = _dot_nt(keys_ref[0, 0], q[:, :PEER_HALF].astype(BF16))
    s1 = _dot_nt(keys_ref[0, 1], q[:, PEER_HALF:].astype(BF16))

    def tables(exact):
        kidx = lax.broadcasted_iota(jnp.int32, s0.shape, 0).astype(F32)
        _, r0, a, n0 = _take_top16(s0, kidx, exact, want_steps=exact)
        _, r1, b, n1 = _take_top16(s1, kidx, exact, want_steps=True)

        cand, pidx, qidx, valid = _candidate_pairs(a, b)
        pos = jnp.broadcast_to((pidx * PEER_TOPK + qidx).astype(F32), cand.shape)
        taken, _, _, n2 = _take_top16(jnp.where(valid, cand, -jnp.inf), pos, exact, want_steps=False)
        chosen = jnp.where(jnp.logical_and(valid, taken), 1.0, 0.0)

        top = a[0:1] + b[0:1]
        gsel = chosen * jnp.exp(jnp.where(valid, cand, top) - top)
        zinv = 1.0 / jnp.sum(gsel, axis=0, keepdims=True)

        c0 = jnp.zeros(s0.shape, F32)
        for p in range(PEER_TOPK):
            cnt = jnp.sum(jnp.where(pidx == p, chosen, 0.0), axis=0, keepdims=True)
            has_rank_p = (r0 == float(p)) if exact else (s0 == a[p:p + 1])
            c0 = jnp.where(has_rank_p, cnt, c0)

        c0_ref[0] = c0
        ap_ref[0] = jnp.exp(s0 - a[0:1]) * (zinv * math.sqrt(0.5))
        r1_ref[0] = (r1 * RANK_SCALE).astype(BF16)
        bp_ref[0] = jnp.exp(s1 - b[0:1]).astype(BF16)
        return n0 + n1 + n2

    taken_total = tables(exact=False)
    tied = jnp.max(jnp.abs(taken_total - 3.0 * PEER_TOPK)) > 0.0

    @pl.when(tied)
    def _():
        tables(exact=True)


def _peer_route(x2, nw, wq, keys, *, tt):
    t = x2.shape[0]
    kern = functools.partial(_route_kernel, tt=tt)
    tab = lambda dt: jax.ShapeDtypeStruct((PEER_HEADS, PEER_KEYS, t), dt)
    tab_spec = pl.BlockSpec((1, PEER_KEYS, tt), lambda i, h: (h, 0, i))
    return pl.pallas_call(
        kern,
        out_shape=(jax.ShapeDtypeStruct((D_MODEL, t), BF16), tab(F32), tab(F32), tab(BF16), tab(BF16)),
        grid=(t // tt, PEER_HEADS),
        in_specs=[pl.BlockSpec((tt, D_MODEL), lambda i, h: (i, 0)),
                  pl.BlockSpec((1, D_MODEL), lambda i, h: (0, 0)),
                  pl.BlockSpec((D_MODEL, 2 * PEER_HALF), lambda i, h: (0, h)),
                  pl.BlockSpec((1, 2, PEER_KEYS, PEER_HALF), lambda i, h: (h, 0, 0, 0))],
        out_specs=(pl.BlockSpec((D_MODEL, tt), lambda i, h: (0, i)),
                   tab_spec, tab_spec, tab_spec, tab_spec),
        scratch_shapes=[pltpu.VMEM((tt, D_MODEL), BF16)],
        compiler_params=pltpu.CompilerParams(dimension_semantics=("parallel", "arbitrary"),
                                             vmem_limit_bytes=VMEM_LIMIT),
        name="peer_route",
    )(x2, nw, wq, keys)


def _peer_eval_kernel(xnt_ref, x_ref, u_ref, v_ref, c0_ref, ap_ref, r1_ref, bp_ref, fnw_ref,
                      out_ref, acc_ref, w_scr, *, eb, sb):
    e = pl.program_id(1)

    @pl.when(e == 0)
    def _():
        acc_ref[...] = jnp.zeros_like(acc_ref)

    tt = xnt_ref.shape[1]
    nsub = eb // sb
    ipb = sb // PEER_KEYS

    def activations(s):
        return _dot(u_ref[s * sb:(s + 1) * sb, :], xnt_ref[...])

    def row_tile(ref, h, il, lanes, scale):
        row = ref[h, il:il + 1, lanes] * scale
        return jnp.broadcast_to(row, (BF16_ROWS, LANES)).astype(BF16)[None]

    def weight_slab(s, j):
        ii, lg = divmod(j, tt // LANES)
        il = s * ipb + ii
        lanes = slice(lg * LANES, (lg + 1) * LANES)
        w = None
        for h in range(PEER_HEADS):
            gate = jnp.clip(row_tile(c0_ref, h, il, lanes, RANK_SCALE) - r1_ref[h, :, :, lanes],
                            0.0, row_tile(ap_ref, h, il, lanes, 1.0))
            term = gate * bp_ref[h, :, :, lanes]
            w = term if w is None else w + term
        w_scr[ii * PEER_KEYS:(ii + 1) * PEER_KEYS, lanes] = w.reshape(PEER_KEYS, LANES)

    act = activations(0)
    part = None
    for s in range(nsub):
        act_next = activations(s + 1) if s + 1 < nsub else None
        for j in range(ipb * (tt // LANES)):
            weight_slab(s, j)
        z = act.astype(BF16)
        pt = w_scr[...] * (z * (1.0 + lax.erf(z)))
        d = _dot_tn(pt, v_ref[s * sb:(s + 1) * sb, :])
        part = d if part is None else part + d
        act = act_next
    acc_ref[...] += part

    @pl.when(e == pl.num_programs(1) - 1)
    def _():
        out_ref[...] = _rms(x_ref[...] + acc_ref[...], fnw_ref[...])


def _peer_eval(xn, x2, u16, v16, c0, ap, r1, bp, fnw, *, tt, eb, sb):
    t = x2.shape[0]
    ne = u16.shape[0]
    ib = eb // PEER_KEYS
    kern = functools.partial(_peer_eval_kernel, eb=eb, sb=sb)
    half0 = pl.BlockSpec((PEER_HEADS, ib, tt), lambda i, e: (0, e, i))
    ktiles = PEER_KEYS // BF16_ROWS
    r1 = r1.reshape(PEER_HEADS, ktiles, BF16_ROWS, t)
    bp = bp.reshape(PEER_HEADS, ktiles, BF16_ROWS, t)
    half1 = pl.BlockSpec((PEER_HEADS, ktiles, BF16_ROWS, tt), lambda i, e: (0, 0, 0, i))
    return pl.pallas_call(
        kern,
        out_shape=jax.ShapeDtypeStruct((t, D_MODEL), F32),
        grid=(t // tt, ne // eb),
        in_specs=[pl.BlockSpec((D_MODEL, tt), lambda i, e: (0, i)),
                  pl.BlockSpec((tt, D_MODEL), lambda i, e: (i, 0)),
                  pl.BlockSpec((eb, D_MODEL), lambda i, e: (e, 0)),
                  pl.BlockSpec((eb, D_MODEL), lambda i, e: (e, 0)),
                  half0, half0, half1, half1,
                  pl.BlockSpec((1, D_MODEL), lambda i, e: (0, 0))],
        out_specs=pl.BlockSpec((tt, D_MODEL), lambda i, e: (i, 0)),
        scratch_shapes=[pltpu.VMEM((tt, D_MODEL), F32), pltpu.VMEM((sb, tt), BF16)],
        compiler_params=pltpu.CompilerParams(dimension_semantics=("parallel", "arbitrary"),
                                             vmem_limit_bytes=VMEM_LIMIT),
        name="peer_eval",
    )(xn, x2, u16, v16, c0, ap, r1, bp, fnw)


def kernel(x, mem, norm_mix_w, w_in, gdn_conv_w, gdn_a_log, gdn_dt_bias, gdn_norm_w, mlstm_conv_w,
           mlstm_i_bias, mlstm_f_bias, mlstm_norm_w, w_out, norm_xa_w, norm_mem_w, xa_wq, xa_wkv,
           xa_wo, norm_ffn_w, peer_wq, peer_sub_keys, peer_u, peer_v, norm_final_w):
    batch, seq, d = x.shape
    t = batch * seq
    assert d == D_MODEL and w_in.shape[0] == 1, "single-layer block; the final norm is fused into it"
    tm = min(256, seq)
    ts = min(256, seq)
    tt = min(512, t)
    xs = x.reshape(t, d)
    mem2d = mem.reshape(batch * MEM_TOKENS, d)

    wl = w_in[0]
    o_gz = 3 * MIX_W
    o_ga = o_gz + MIX_W
    o_gb = o_ga + MIX_HEADS
    o_mqk = o_gb + MIX_HEADS
    o_mv = o_mqk + 2 * MIX_W
    o_mo = o_mv + MIX_W
    o_mi = o_mo + MIX_W
    o_mf = o_mi + MIX_HEADS
    w_all = jnp.concatenate([wl[:, 0:o_gz], wl[:, o_mqk:o_mv], wl[:, o_gz:o_ga],
                             wl[:, o_mv:o_mo], wl[:, o_mo:o_mi],
                             wl[:, o_ga:o_mqk], wl[:, o_mi:o_mf + MIX_HEADS],
                             jnp.zeros((d, GATE_COLS - 4 * MIX_HEADS), wl.dtype)], axis=1).astype(BF16)
    cw = jnp.concatenate([gdn_conv_w[0], mlstm_conv_w[0]], axis=1).astype(F32)
    zeros4 = jnp.zeros((MIX_HEADS,), F32)
    pad = jnp.zeros((GATE_COLS - 4 * MIX_HEADS,), F32)
    gbias = jnp.concatenate([gdn_dt_bias[0], zeros4, mlstm_i_bias[0], mlstm_f_bias[0], pad]
                            ).astype(F32).reshape(1, GATE_COLS)
    alog = jnp.concatenate([gdn_a_log[0], zeros4, zeros4, zeros4, pad]).astype(F32).reshape(1, GATE_COLS)

    kmem, vmem = _memkv(mem2d, norm_mem_w[0].reshape(1, d), xa_wkv[0].astype(BF16))
    qkv, zvo, gates = _inproj(xs, norm_mix_w[0].reshape(1, d), w_all, cw, gbias, alog,
                              seq=seq, tm=tm)
    go = _gdn(qkv, gates, batch=batch, seq=seq, ts=ts)
    mh = _mlstm(qkv, zvo, gates, batch=batch, seq=seq, ts=ts)
    x2 = _mix_xattn(xs, go, mh, zvo, gdn_norm_w[0].reshape(1, HEAD_DIM),
                    mlstm_norm_w[0].reshape(1, MIX_W), w_out[0].astype(BF16),
                    norm_xa_w[0].reshape(1, d), xa_wq[0].astype(BF16),
                    kmem.reshape(batch, MEM_TOKENS, d), vmem.reshape(batch, MEM_TOKENS, d),
                    xa_wo[0].astype(BF16), seq=seq, tm=tm)
    xn, c0, ap, r1, bp = _peer_route(x2, norm_ffn_w[0].reshape(1, d), peer_wq[0].astype(BF16),
                                     peer_sub_keys[0].astype(BF16), tt=min(2 * tt, t))
    u16 = (peer_u[0] * math.sqrt(0.5)).astype(BF16)
    out = _peer_eval(xn, x2, u16, peer_v[0].astype(BF16), c0, ap, r1, bp,
                     norm_final_w.reshape(1, d), tt=tt, eb=4096, sb=256)
    return out.reshape(batch, seq, d)
```

```python
import functools
import math

import jax
import jax.numpy as jnp
from jax import lax
from jax.experimental import pallas as pl
from jax.experimental.pallas import tpu as pltpu

F32 = jnp.float32
BF16 = jnp.bfloat16
HIGHEST = lax.Precision.HIGHEST

D_MODEL = 1024
HEAD_DIM = 128
MIX_HEADS = 4
MIX_W = MIX_HEADS * HEAD_DIM
CONV_WIDTH = 4
CHUNK = 64
MEM_TOKENS = 256
XA_HEADS = 4
XA_HEAD_DIM = D_MODEL // XA_HEADS
PEER_HEADS = 8
PEER_KEYS = 128
PEER_TOPK = 16
PEER_HALF = 128
NORM_EPS = 1e-6
NEG_BIG = -1e30
LANES = 128
MXU_COLS = 256
BF16_ROWS = 16
RANK_SCALE = 2.0

CONV_COLS = 3 * MIX_W + 2 * MIX_W
PLAIN_COLS = 3 * MIX_W
GATE_COLS = 128
PROJ_COLS = CONV_COLS + PLAIN_COLS + GATE_COLS
HALO = 8

VMEM_LIMIT = 60 * 1024 * 1024

NT_DIMS = (((1,), (1,)), ((), ()))
TN_DIMS = (((0,), (0,)), ((), ()))


def _rms(x, w):
    return x * lax.rsqrt(jnp.mean(x * x, axis=-1, keepdims=True) + NORM_EPS) * w


def _sigmoid(x):
    return 1.0 / (1.0 + jnp.exp(-x))


def _softplus(x):
    return jnp.maximum(x, 0.0) + jnp.log1p(jnp.exp(-jnp.abs(x)))


def _dot(a, b):
    return jnp.dot(a, b, preferred_element_type=F32)


def _dot_nt(a, b, precision=None):
    return lax.dot_general(a, b, NT_DIMS, precision=precision, preferred_element_type=F32)


def _dot_tn(a, b):
    return lax.dot_general(a, b, TN_DIMS, preferred_element_type=F32)


def _memkv_kernel(m_ref, nw_ref, w_ref, k_ref, v_ref):
    mn = _rms(m_ref[...], nw_ref[...]).astype(BF16)
    kv = _dot(mn, w_ref[...])
    k_ref[...] = kv[:, :D_MODEL].astype(BF16)
    v_ref[...] = kv[:, D_MODEL:].astype(BF16)


def _memkv(mem2d, nw, wkv):
    n = mem2d.shape[0]
    tm = MEM_TOKENS
    return pl.pallas_call(
        _memkv_kernel,
        out_shape=(jax.ShapeDtypeStruct((n, D_MODEL), BF16),
                   jax.ShapeDtypeStruct((n, D_MODEL), BF16)),
        grid=(n // tm,),
        in_specs=[pl.BlockSpec((tm, D_MODEL), lambda i: (i, 0)),
                  pl.BlockSpec((1, D_MODEL), lambda i: (0, 0)),
                  pl.BlockSpec((D_MODEL, 2 * D_MODEL), lambda i: (0, 0))],
        out_specs=(pl.BlockSpec((tm, D_MODEL), lambda i: (i, 0)),
                   pl.BlockSpec((tm, D_MODEL), lambda i: (i, 0))),
        compiler_params=pltpu.CompilerParams(dimension_semantics=("parallel",),
                                             vmem_limit_bytes=VMEM_LIMIT),
        name="memkv",
    )(mem2d, nw, wkv)


def _inproj_kernel(x_ref, halo_ref, nw_ref, w_ref, cw_ref, gbias_ref, alog_ref,
                   qkv_ref, zvo_ref, gate_ref, p_scr, *, tm, seq):
    i = pl.program_id(0)
    xc = jnp.concatenate([halo_ref[...], x_ref[...]], axis=0)
    xn = _rms(xc, nw_ref[...]).astype(BF16)
    p = _dot(xn, w_ref[...])

    first_row = jnp.where((i * tm) % seq == 0, HALO, 0)
    row = lax.broadcasted_iota(jnp.int32, (HALO, 1), 0)
    p_scr[:HALO, :] = jnp.where(row >= first_row, p[:HALO, :CONV_COLS], 0.0)
    p_scr[HALO:, :] = p[HALO:, :CONV_COLS]
    cw = cw_ref[...]
    y = cw[CONV_WIDTH - 1:CONV_WIDTH] * p_scr[HALO:HALO + tm, :]
    for kk in range(CONV_WIDTH - 1):
        off = HALO - (CONV_WIDTH - 1) + kk
        y = y + cw[kk:kk + 1] * p_scr[off:off + tm, :]
    y = y * _sigmoid(y)

    scale = HEAD_DIM ** -0.5
    for hh in range(2 * MIX_HEADS):
        cols = slice(hh * HEAD_DIM, (hh + 1) * HEAD_DIM)
        yh = y[:, cols]
        inv = lax.rsqrt(jnp.sum(yh * yh, axis=-1, keepdims=True) + NORM_EPS)
        if hh < MIX_HEADS:
            inv = inv * scale
        qkv_ref[:, cols] = yh * inv
    qkv_ref[:, 2 * MIX_W:4 * MIX_W] = y[:, 2 * MIX_W:4 * MIX_W]
    qkv_ref[:, 4 * MIX_W:5 * MIX_W] = y[:, 4 * MIX_W:5 * MIX_W] * scale

    zvo_ref[...] = p[HALO:, CONV_COLS:CONV_COLS + PLAIN_COLS]

    ga = p[HALO:, CONV_COLS + PLAIN_COLS:] + gbias_ref[...]
    lane = lax.broadcasted_iota(jnp.int32, (tm, GATE_COLS), 1)
    log_alpha = -jnp.exp(alog_ref[...]) * _softplus(ga)
    beta = _sigmoid(ga)
    log_f = -_softplus(-ga)
    gate_ref[...] = jnp.where(lane < 4, log_alpha,
                              jnp.where(lane < 8, beta,
                                        jnp.where(lane < 12, ga,
                                                  jnp.where(lane < 16, log_f, 0.0))))


def _inproj(x2d, nw, w_all, cw, gbias, alog, *, seq, tm):
    t = x2d.shape[0]
    hb = tm // HALO
    kern = functools.partial(_inproj_kernel, tm=tm, seq=seq)
    return pl.pallas_call(
        kern,
        out_shape=(jax.ShapeDtypeStruct((t, CONV_COLS), F32),
                   jax.ShapeDtypeStruct((t, PLAIN_COLS), F32),
                   jax.ShapeDtypeStruct((t, GATE_COLS), F32)),
        grid=(t // tm,),
        in_specs=[pl.BlockSpec((tm, D_MODEL), lambda i: (i, 0)),
                  pl.BlockSpec((HALO, D_MODEL), lambda i: (jnp.maximum(i * hb - 1, 0), 0)),
                  pl.BlockSpec((1, D_MODEL), lambda i: (0, 0)),
                  pl.BlockSpec((D_MODEL, PROJ_COLS), lambda i: (0, 0)),
                  pl.BlockSpec((CONV_WIDTH, CONV_COLS), lambda i: (0, 0)),
                  pl.BlockSpec((1, GATE_COLS), lambda i: (0, 0)),
                  pl.BlockSpec((1, GATE_COLS), lambda i: (0, 0))],
        out_specs=(pl.BlockSpec((tm, CONV_COLS), lambda i: (i, 0)),
                   pl.BlockSpec((tm, PLAIN_COLS), lambda i: (i, 0)),
                   pl.BlockSpec((tm, GATE_COLS), lambda i: (i, 0))),
        scratch_shapes=[pltpu.VMEM((HALO + tm, CONV_COLS), F32)],
        compiler_params=pltpu.CompilerParams(dimension_semantics=("parallel",),
                                             vmem_limit_bytes=VMEM_LIMIT),
        name="inproj",
    )(x2d, x2d, nw, w_all, cw, gbias, alog)


def _chunk_cumsum(g, ts):
    ri = lax.broadcasted_iota(jnp.int32, (ts, ts), 0)
    ci = lax.broadcasted_iota(jnp.int32, (ts, ts), 1)
    shift = CHUNK.bit_length() - 1
    same_chunk = jnp.right_shift(ri, shift) == jnp.right_shift(ci, shift)
    tri = jnp.where(jnp.logical_and(same_chunk, ci <= ri), 1.0, 0.0)
    return jnp.dot(tri, g, precision=HIGHEST, preferred_element_type=F32)


def _lane_pick(x, lane, idx):
    return jnp.where(lane == idx, x, 0.0)


def _gdn_kernel(q_ref, k_ref, v_ref, g_ref, o_ref, s_ref, *, ts):
    @pl.when(pl.program_id(1) == 0)
    def _():
        s_ref[...] = jnp.zeros_like(s_ref)

    L = CHUNK
    nch = ts // L
    g = g_ref[...]
    cs = _chunk_cumsum(g, ts)
    lane = lax.broadcasted_iota(jnp.int32, (L, GATE_COLS), 1)
    r = lax.broadcasted_iota(jnp.int32, (L, L), 0)
    c = lax.broadcasted_iota(jnp.int32, (L, L), 1)
    causal = r >= c
    strict = r > c
    eye = jnp.where(r == c, 1.0, 0.0)

    items = [(ch, h) for ch in range(nch) for h in range(MIX_HEADS)]
    n = len(items)

    def rows_of(ch):
        return slice(ch * L, (ch + 1) * L)

    def cols_of(h):
        return slice(h * HEAD_DIM, (h + 1) * HEAD_DIM)

    q = [q_ref[rows_of(ch), cols_of(h)] for ch, h in items]
    k = [k_ref[rows_of(ch), cols_of(h)] for ch, h in items]
    v = [v_ref[rows_of(ch), cols_of(h)] for ch, h in items]
    gc = [cs[rows_of(ch), h:h + 1] for ch, h in items]
    beta = [g[rows_of(ch), 4 + h:5 + h] for ch, h in items]
    gl = [cs[(ch + 1) * L - 1:(ch + 1) * L, h:h + 1] for ch, h in items]

    diff = []
    for ch, h in items:
        e_h = jnp.where(lane == h, 1.0, 0.0)
        cs_h = _lane_pick(cs[rows_of(ch)], lane, h)
        diff.append(_dot_nt(jnp.concatenate([cs_h, e_h], axis=1),
                            jnp.concatenate([e_h, -cs_h], axis=1), precision=HIGHEST))
    decay = [jnp.exp(jnp.where(causal, d, NEG_BIG)) for d in diff]
    eg = [jnp.exp(x) for x in gc]
    kb = [k[i] * beta[i] for i in range(n)]
    k16 = [x.astype(BF16) for x in k]
    q16 = [x.astype(BF16) for x in q]
    kk = [_dot_nt(kb[i].astype(BF16), k16[i]) for i in range(n)]
    qk = [_dot_nt(q16[i], k16[i]) for i in range(n)]
    a = [jnp.where(strict, kk[i] * decay[i], 0.0) for i in range(n)]
    qk16 = [(qk[i] * decay[i]).astype(BF16) for i in range(n)]
    tinv = [eye - x for x in a]
    pw = a
    for _ in range(5):
        pw16 = [x.astype(BF16) for x in pw]
        pw = [_dot(x, x) for x in pw16]
        tinv = [tinv[i] + _dot(tinv[i].astype(BF16), pw[i].astype(BF16)) for i in range(n)]
    rhs = [jnp.concatenate([v[i] * beta[i], kb[i] * eg[i]], axis=1).astype(BF16) for i in range(n)]
    uw = [_dot(tinv[i].astype(BF16), rhs[i]) for i in range(n)]
    wq16 = [jnp.concatenate([uw[i][:, HEAD_DIM:], q[i] * eg[i]], axis=0).astype(BF16)
            for i in range(n)]
    kd16 = [(k[i] * jnp.exp(gl[i] - gc[i])).astype(BF16) for i in range(n)]
    egl = [jnp.exp(x) for x in gl]

    state = [s_ref[h] for h in range(MIX_HEADS)]
    for ch in range(nch):
        base = ch * MIX_HEADS
        rr = [_dot(wq16[base + h], state[h].astype(BF16)) for h in range(MIX_HEADS)]
        vnew16 = [(uw[base + h][:, :HEAD_DIM] - rr[h][:L]).astype(BF16) for h in range(MIX_HEADS)]
        for h in range(MIX_HEADS):
            o_ref[rows_of(ch), cols_of(h)] = rr[h][L:] + _dot(qk16[base + h], vnew16[h])
        state = [state[h] * egl[base + h] + _dot_tn(kd16[base + h], vnew16[h])
                 for h in range(MIX_HEADS)]
    for h in range(MIX_HEADS):
        s_ref[h] = state[h]


def _gdn(qkv, gates, *, batch, seq, ts):
    t = qkv.shape[0]
    nt = seq // ts
    kern = functools.partial(_gdn_kernel, ts=ts)

    def col(j):
        return pl.BlockSpec((ts, MIX_W), lambda b, s, j=j: (b * nt + s, j))

    return pl.pallas_call(
        kern,
        out_shape=jax.ShapeDtypeStruct((t, MIX_W), F32),
        grid=(batch, nt),
        in_specs=[col(0), col(1), col(2),
                  pl.BlockSpec((ts, GATE_COLS), lambda b, s: (b * nt + s, 0))],
        out_specs=pl.BlockSpec((ts, MIX_W), lambda b, s: (b * nt + s, 0)),
        scratch_shapes=[pltpu.VMEM((MIX_HEADS, HEAD_DIM, HEAD_DIM), F32)],
        compiler_params=pltpu.CompilerParams(dimension_semantics=("parallel", "arbitrary"),
                                             vmem_limit_bytes=VMEM_LIMIT),
        name="gdn",
    )(qkv, qkv, qkv, gates)


def _mlstm_kernel(q_ref, k_ref, v_ref, g_ref, o_ref, c_ref, m_ref, *, ts):
    @pl.when(pl.program_id(1) == 0)
    def _():
        c_ref[...] = jnp.zeros_like(c_ref)
        m_ref[...] = jnp.zeros_like(m_ref)

    L = CHUNK
    nch = ts // L
    g = g_ref[...]
    cs = _chunk_cumsum(g, ts)
    lane = lax.broadcasted_iota(jnp.int32, (L, GATE_COLS), 1)
    r = lax.broadcasted_iota(jnp.int32, (L, L), 0)
    c = lax.broadcasted_iota(jnp.int32, (L, L), 1)
    causal = r >= c
    ones = jnp.ones((L, HEAD_DIM), F32)

    items = [(ch, h) for ch in range(nch) for h in range(MIX_HEADS)]
    n = len(items)

    def rows_of(ch):
        return slice(ch * L, (ch + 1) * L)

    def cols_of(h):
        return slice(h * HEAD_DIM, (h + 1) * HEAD_DIM)

    q16 = [q_ref[rows_of(ch), cols_of(h)].astype(BF16) for ch, h in items]
    k = [k_ref[rows_of(ch), cols_of(h)] for ch, h in items]
    vaug = [jnp.concatenate([v_ref[rows_of(ch), cols_of(h)], ones], axis=1).astype(BF16)
            for ch, h in items]
    bc = [cs[rows_of(ch), 12 + h:13 + h] for ch, h in items]
    li = [g[rows_of(ch), 8 + h:9 + h] for ch, h in items]
    bl = [cs[(ch + 1) * L - 1:(ch + 1) * L, 12 + h:13 + h] for ch, h in items]
    dmat = []
    for ch, h in items:
        csch = cs[rows_of(ch)]
        gch = g[rows_of(ch)]
        e_f = jnp.where(lane == 12 + h, 1.0, 0.0)
        e_if = jnp.where(jnp.logical_or(lane == 8 + h, lane == 12 + h), 1.0, 0.0)
        xm = jnp.concatenate([_lane_pick(csch, lane, 12 + h), e_if], axis=1)
        ym = jnp.concatenate([e_f, _lane_pick(gch, lane, 8 + h) - _lane_pick(csch, lane, 12 + h)],
                             axis=1)
        dmat.append(jnp.where(causal, _dot_nt(xm, ym, precision=HIGHEST), NEG_BIG))
    qk = [_dot_nt(q16[i], k[i].astype(BF16)) for i in range(n)]
    rowmax = [jnp.max(d, axis=1, keepdims=True) for d in dmat]
    dl = [bl[i] - bc[i] + li[i] for i in range(n)]
    dlmax = [jnp.max(x, axis=0, keepdims=True) for x in dl]

    m_in = []
    m_cur = [m_ref[h][0:1, 0:1] for h in range(MIX_HEADS)]
    for i, (ch, h) in enumerate(items):
        m_in.append(m_cur[h])
        m_cur[h] = jnp.maximum(bl[i] + m_cur[h], dlmax[i])
    m_out = [m_cur[h] if ch == nch - 1 else m_in[(ch + 1) * MIX_HEADS + h] for ch, h in items]
    for h in range(MIX_HEADS):
        m_ref[h] = jnp.broadcast_to(m_cur[h], m_ref.shape[1:])

    inter = [bc[i] + m_in[i] for i in range(n)]
    mt = [jnp.maximum(inter[i], rowmax[i]) for i in range(n)]
    p16 = [(qk[i] * jnp.exp(dmat[i] - mt[i])).astype(BF16) for i in range(n)]
    sc = [jnp.exp(inter[i] - mt[i]) for i in range(n)]
    pv = [_dot(p16[i], vaug[i]) for i in range(n)]
    kw16 = [(k[i] * jnp.exp(dl[i] - m_out[i])).astype(BF16) for i in range(n)]
    kv = [_dot_tn(kw16[i], vaug[i]) for i in range(n)]
    dec = [jnp.exp(bl[i] + m_in[i] - m_out[i]) for i in range(n)]

    cst = [c_ref[h] for h in range(MIX_HEADS)]
    c_in = []
    for i, (ch, h) in enumerate(items):
        c_in.append(cst[h].astype(BF16))
        cst[h] = dec[i] * cst[h] + kv[i]
    for h in range(MIX_HEADS):
        c_ref[h] = cst[h]

    for i, (ch, h) in enumerate(items):
        nd = sc[i] * _dot(q16[i], c_in[i]) + pv[i]
        den = jnp.maximum(jnp.abs(nd[:, HEAD_DIM:]), jnp.exp(-mt[i]))
        o_ref[rows_of(ch), cols_of(h)] = nd[:, :HEAD_DIM] / den


def _mlstm(qkv, zvo, gates, *, batch, seq, ts):
    t = qkv.shape[0]
    nt = seq // ts
    kern = functools.partial(_mlstm_kernel, ts=ts)

    def col(j):
        return pl.BlockSpec((ts, MIX_W), lambda b, s, j=j: (b * nt + s, j))

    return pl.pallas_call(
        kern,
        out_shape=jax.ShapeDtypeStruct((t, MIX_W), F32),
        grid=(batch, nt),
        in_specs=[col(3), col(4), col(1),
                  pl.BlockSpec((ts, GATE_COLS), lambda b, s: (b * nt + s, 0))],
        out_specs=pl.BlockSpec((ts, MIX_W), lambda b, s: (b * nt + s, 0)),
        scratch_shapes=[pltpu.VMEM((MIX_HEADS, HEAD_DIM, 2 * HEAD_DIM), F32),
                        pltpu.VMEM((MIX_HEADS, 8, 128), F32)],
        compiler_params=pltpu.CompilerParams(dimension_semantics=("parallel", "arbitrary"),
                                             vmem_limit_bytes=VMEM_LIMIT),
        name="mlstm",
    )(qkv, qkv, zvo, gates)


def _mix_xattn_kernel(x_ref, go_ref, mh_ref, z_ref, mo_ref, gnw_ref, mnw_ref, wout_ref,
                      nxa_ref, wq_ref, k_ref, v_ref, wo_ref, out_ref):
    parts = []
    gnw = gnw_ref[...]
    for h in range(MIX_HEADS):
        cols = slice(h * HEAD_DIM, (h + 1) * HEAD_DIM)
        z = z_ref[:, cols]
        parts.append(_rms(go_ref[:, cols], gnw) * (z * _sigmoid(z)))
    for h in range(MIX_HEADS):
        cols = slice(h * HEAD_DIM, (h + 1) * HEAD_DIM)
        parts.append(_rms(mh_ref[:, cols], mnw_ref[:, cols]) * _sigmoid(mo_ref[:, cols]))
    mixed = jnp.concatenate(parts, axis=1).astype(BF16)
    x1 = x_ref[...] + _dot(mixed, wout_ref[...])

    xn = _rms(x1, nxa_ref[...]).astype(BF16)
    q = _dot(xn, wq_ref[...])
    kk = k_ref[0]
    vv = v_ref[0]
    heads = [slice(h * XA_HEAD_DIM, (h + 1) * XA_HEAD_DIM) for h in range(XA_HEADS)]
    s = [_dot_nt(q[:, c].astype(BF16), kk[:, c]) * (XA_HEAD_DIM ** -0.5) for c in heads]
    e = [jnp.exp(x - jnp.max(x, axis=-1, keepdims=True)) for x in s]
    inv = [1.0 / jnp.sum(x, axis=-1, keepdims=True) for x in e]
    pv = [_dot(e[h].astype(BF16), vv[:, heads[h]]) for h in range(XA_HEADS)]
    o = jnp.concatenate([pv[h] * inv[h] for h in range(XA_HEADS)], axis=1).astype(BF16)
    out_ref[...] = x1 + _dot(o, wo_ref[...])


def _mix_xattn(x2d, go, mh, zvo, gnw, mnw, wout, nxa, wq, kmem, vmem, wo, *, seq, tm):
    t = x2d.shape[0]
    per_seq = seq // tm

    def row(width, j=0):
        return pl.BlockSpec((tm, width), lambda i, j=j: (i, j))

    def const(shape):
        return pl.BlockSpec(shape, lambda i: (0,) * len(shape))

    mem_spec = pl.BlockSpec((1, MEM_TOKENS, D_MODEL), lambda i: (i // per_seq, 0, 0))
    return pl.pallas_call(
        _mix_xattn_kernel,
        out_shape=jax.ShapeDtypeStruct((t, D_MODEL), F32),
        grid=(t // tm,),
        in_specs=[row(D_MODEL), row(MIX_W), row(MIX_W), row(MIX_W, 0), row(MIX_W, 2),
                  const((1, HEAD_DIM)), const((1, MIX_W)), const((D_MODEL, D_MODEL)),
                  const((1, D_MODEL)), const((D_MODEL, D_MODEL)), mem_spec, mem_spec,
                  const((D_MODEL, D_MODEL))],
        out_specs=row(D_MODEL),
        compiler_params=pltpu.CompilerParams(dimension_semantics=("parallel",),
                                             vmem_limit_bytes=VMEM_LIMIT),
        name="mix_xattn",
    )(x2d, go, mh, zvo, zvo, gnw, mnw, wout, nxa, wq, kmem, vmem, wo)


def _take_top16(values, order, exact, want_steps):
    tt = values.shape[1]
    slot = lax.broadcasted_iota(jnp.int32, (PEER_TOPK, tt), 0)
    v = values
    step = jnp.full(values.shape, float(PEER_TOPK), F32) if want_steps else None
    vals = jnp.zeros((PEER_TOPK, tt), F32)
    for kk in range(PEER_TOPK):
        m = jnp.max(v, axis=0, keepdims=True)
        if exact:
            first = jnp.min(jnp.where(v == m, order, 1e9), axis=0, keepdims=True)
            sel = order == first
        else:
            sel = v == m
        if want_steps:
            step = jnp.where(sel, float(kk), step)
        v = jnp.where(sel, -jnp.inf, v)
        vals = jnp.where(slot == kk, m, vals)
    taken = v < values
    count = jnp.sum(jnp.where(taken, 1.0, 0.0), axis=0, keepdims=True)
    return taken, step, vals, count


def _candidate_pairs(a, b):
    tt = a.shape[1]
    sub = lax.broadcasted_iota(jnp.int32, (8, 1), 0)
    zero = jnp.zeros((8, 1), jnp.int32)

    def row(x, p):
        return jnp.broadcast_to(x[p:p + 1], (8, tt))

    b_lo4 = jnp.where(sub < 4, b[0:8], pltpu.roll(b[0:8], 4, 0))
    b_01 = jnp.where(sub % 2 == 0, row(b, 0), row(b, 1))
    groups = (
        (row(a, 0), b[0:8], zero, sub, sub < 8),
        (row(a, 0), b[8:16], zero, sub + 8, sub < 8),
        (row(a, 1), b[0:8], zero + 1, sub, sub < 8),
        (row(a, 2), b[0:8], zero + 2, sub, sub < 5),
        (jnp.where(sub < 4, row(a, 3), row(a, 4)), b_lo4,
         jnp.where(sub < 4, 3, 4), sub % 4, sub < 7),
        (jnp.where(sub < 2, row(a, 5), jnp.where(sub < 4, row(a, 6), row(a, 7))), b_01,
         5 + sub // 2, sub % 2, sub < 6),
        (a[8:16], row(b, 0), sub + 8, zero, sub < 8),
    )
    cand = jnp.concatenate([g[0] + g[1] for g in groups], axis=0)
    pidx = jnp.concatenate([g[2] for g in groups], axis=0)
    qidx = jnp.concatenate([g[3] for g in groups], axis=0)
    valid = jnp.concatenate([g[4] for g in groups], axis=0)
    return cand, pidx, qidx, valid


def _route_kernel(x_ref, nw_ref, wq_ref, keys_ref, xnt_ref, c0_ref, ap_ref, r1_ref, bp_ref,
                  xn_scr, *, tt):
    @pl.when(pl.program_id(1) == 0)
    def _():
        xn32 = _rms(x_ref[...], nw_ref[...])
        xn_scr[...] = xn32.astype(BF16)
        xnt_ref[...] = xn32.T.astype(BF16)

    xn = xn_scr[...]
    q = _dot(xn, wq_ref[...])
    s0 = _dot_nt(keys_ref[0, 0], q[:, :PEER_HALF].astype(BF16))
    s1 = _dot_nt(keys_ref[0, 1], q[:, PEER_HALF:].astype(BF16))

    def tables(exact):
        kidx = lax.broadcasted_iota(jnp.int32, s0.shape, 0).astype(F32)
        _, r0, a, n0 = _take_top16(s0, kidx, exact, want_steps=exact)
        _, r1, b, n1 = _take_top16(s1, kidx, exact, want_steps=True)

        cand, pidx, qidx, valid = _candidate_pairs(a, b)
        pos = jnp.broadcast_to((pidx * PEER_TOPK + qidx).astype(F32), cand.shape)
        taken, _, _, n2 = _take_top16(jnp.where(valid, cand, -jnp.inf), pos, exact, want_steps=False)
        chosen = jnp.where(jnp.logical_and(valid, taken), 1.0, 0.0)

        top = a[0:1] + b[0:1]
        gsel = chosen * jnp.exp(jnp.where(valid, cand, top) - top)
        zinv = 1.0 / jnp.sum(gsel, axis=0, keepdims=True)

        c0 = jnp.zeros(s0.shape, F32)
        for p in range(PEER_TOPK):
            cnt = jnp.sum(jnp.where(pidx == p, chosen, 0.0), axis=0, keepdims=True)
            has_rank_p = (r0 == float(p)) if exact else (s0 == a[p:p + 1])
            c0 = jnp.where(has_rank_p, cnt, c0)

        c0_ref[0] = c0
        ap_ref[0] = jnp.exp(s0 - a[0:1]) * (zinv * math.sqrt(0.5))
        r1_ref[0] = (r1 * RANK_SCALE).astype(BF16)
        bp_ref[0] = jnp.exp(s1 - b[0:1]).astype(BF16)
        return n0 + n1 + n2

    taken_total = tables(exact=False)
    tied = jnp.max(jnp.abs(taken_total - 3.0 * PEER_TOPK)) > 0.0

    @pl.when(tied)
    def _():
        tables(exact=True)


def _peer_route(x2, nw, wq, keys, *, tt):
    t = x2.shape[0]
    kern = functools.partial(_route_kernel, tt=tt)
    tab = lambda dt: jax.ShapeDtypeStruct((PEER_HEADS, PEER_KEYS, t), dt)
    tab_spec = pl.BlockSpec((1, PEER_KEYS, tt), lambda i, h: (h, 0, i))
    return pl.pallas_call(
        kern,
        out_shape=(jax.ShapeDtypeStruct((D_MODEL, t), BF16), tab(F32), tab(F32), tab(BF16), tab(BF16)),
        grid=(t // tt, PEER_HEADS),
        in_specs=[pl.BlockSpec((tt, D_MODEL), lambda i, h: (i, 0)),
                  pl.BlockSpec((1, D_MODEL), lambda i, h: (0, 0)),
                  pl.BlockSpec((D_MODEL, 2 * PEER_HALF), lambda i, h: (0, h)),
                  pl.BlockSpec((1, 2, PEER_KEYS, PEER_HALF), lambda i, h: (h, 0, 0, 0))],
        out_specs=(pl.BlockSpec((D_MODEL, tt), lambda i, h: (0, i)),
                   tab_spec, tab_spec, tab_spec, tab_spec),
        scratch_shapes=[pltpu.VMEM((tt, D_MODEL), BF16)],
        compiler_params=pltpu.CompilerParams(dimension_semantics=("parallel", "arbitrary"),
                                             vmem_limit_bytes=VMEM_LIMIT),
        name="peer_route",
    )(x2, nw, wq, keys)


def _peer_eval_kernel(xnt_ref, x_ref, u_ref, v_ref, c0_ref, ap_ref, r1_ref, bp_ref, fnw_ref,
                      out_ref, acc_ref, w_scr, *, eb, sb):
    e = pl.program_id(1)

    @pl.when(e == 0)
    def _():
        acc_ref[...] = jnp.zeros_like(acc_ref)

    tt = xnt_ref.shape[1]
    nsub = eb // sb
    ipb = sb // PEER_KEYS

    def activations(s):
        return _dot(u_ref[s * sb:(s + 1) * sb, :], xnt_ref[...])

    def row_tile(ref, h, il, lanes, scale):
        row = ref[h, il:il + 1, lanes] * scale
        return jnp.broadcast_to(row, (BF16_ROWS, LANES)).astype(BF16)[None]

    def weight_slab(s, j):
        ii, lg = divmod(j, tt // LANES)
        il = s * ipb + ii
        lanes = slice(lg * LANES, (lg + 1) * LANES)
        w = None
        for h in range(PEER_HEADS):
            gate = jnp.clip(row_tile(c0_ref, h, il, lanes, RANK_SCALE) - r1_ref[h, :, :, lanes],
                            0.0, row_tile(ap_ref, h, il, lanes, 1.0))
            term = gate * bp_ref[h, :, :, lanes]
            w = term if w is None else w + term
        w_scr[ii * PEER_KEYS:(ii + 1) * PEER_KEYS, lanes] = w.reshape(PEER_KEYS, LANES)

    act = activations(0)
    part = None
    for s in range(nsub):
        act_next = activations(s + 1) if s + 1 < nsub else None
        for j in range(ipb * (tt // LANES)):
            weight_slab(s, j)
        z = act.astype(BF16)
        pt = w_scr[...] * (z * (1.0 + lax.erf(z)))
        d = _dot_tn(pt, v_ref[s * sb:(s + 1) * sb, :])
        part = d if part is None else part + d
        act = act_next
    acc_ref[...] += part

    @pl.when(e == pl.num_programs(1) - 1)
    def _():
        out_ref[...] = _rms(x_ref[...] + acc_ref[...], fnw_ref[...])


def _peer_eval(xn, x2, u16, v16, c0, ap, r1, bp, fnw, *, tt, eb, sb):
    t = x2.shape[0]
    ne = u16.shape[0]
    ib = eb // PEER_KEYS
    kern = functools.partial(_peer_eval_kernel, eb=eb, sb=sb)
    half0 = pl.BlockSpec((PEER_HEADS, ib, tt), lambda i, e: (0, e, i))
    ktiles = PEER_KEYS // BF16_ROWS
    r1 = r1.reshape(PEER_HEADS, ktiles, BF16_ROWS, t)
    bp = bp.reshape(PEER_HEADS, ktiles, BF16_ROWS, t)
    half1 = pl.BlockSpec((PEER_HEADS, ktiles, BF16_ROWS, tt), lambda i, e: (0, 0, 0, i))
    return pl.pallas_call(
        kern,
        out_shape=jax.ShapeDtypeStruct((t, D_MODEL), F32),
        grid=(t // tt, ne // eb),
        in_specs=[pl.BlockSpec((D_MODEL, tt), lambda i, e: (0, i)),
                  pl.BlockSpec((tt, D_MODEL), lambda i, e: (i, 0)),
                  pl.BlockSpec((eb, D_MODEL), lambda i, e: (e, 0)),
                  pl.BlockSpec((eb, D_MODEL), lambda i, e: (e, 0)),
                  half0, half0, half1, half1,
                  pl.BlockSpec((1, D_MODEL), lambda i, e: (0, 0))],
        out_specs=pl.BlockSpec((tt, D_MODEL), lambda i, e: (i, 0)),
        scratch_shapes=[pltpu.VMEM((tt, D_MODEL), F32), pltpu.VMEM((sb, tt), BF16)],
        compiler_params=pltpu.CompilerParams(dimension_semantics=("parallel", "arbitrary"),
                                             vmem_limit_bytes=VMEM_LIMIT),
        name="peer_eval",
    )(xn, x2, u16, v16, c0, ap, r1, bp, fnw)


def kernel(x, mem, norm_mix_w, w_in, gdn_conv_w, gdn_a_log, gdn_dt_bias, gdn_norm_w, mlstm_conv_w,
           mlstm_i_bias, mlstm_f_bias, mlstm_norm_w, w_out, norm_xa_w, norm_mem_w, xa_wq, xa_wkv,
           xa_wo, norm_ffn_w, peer_wq, peer_sub_keys, peer_u, peer_v, norm_final_w):
    batch, seq, d = x.shape
    t = batch * seq
    assert d == D_MODEL and w_in.shape[0] == 1, "single-layer block; the final norm is fused into it"
    tm = min(256, seq)
    ts = min(256, seq)
    tt = min(512, t)
    xs = x.reshape(t, d)
    mem2d = mem.reshape(batch * MEM_TOKENS, d)

    wl = w_in[0]
    o_gz = 3 * MIX_W
    o_ga = o_gz + MIX_W
    o_gb = o_ga + MIX_HEADS
    o_mqk = o_gb + MIX_HEADS
    o_mv = o_mqk + 2 * MIX_W
    o_mo = o_mv + MIX_W
    o_mi = o_mo + MIX_W
    o_mf = o_mi + MIX_HEADS
    w_all = jnp.concatenate([wl[:, 0:o_gz], wl[:, o_mqk:o_mv], wl[:, o_gz:o_ga],
                             wl[:, o_mv:o_mo], wl[:, o_mo:o_mi],
                             wl[:, o_ga:o_mqk], wl[:, o_mi:o_mf + MIX_HEADS],
                             jnp.zeros((d, GATE_COLS - 4 * MIX_HEADS), wl.dtype)], axis=1).astype(BF16)
    cw = jnp.concatenate([gdn_conv_w[0], mlstm_conv_w[0]], axis=1).astype(F32)
    zeros4 = jnp.zeros((MIX_HEADS,), F32)
    pad = jnp.zeros((GATE_COLS - 4 * MIX_HEADS,), F32)
    gbias = jnp.concatenate([gdn_dt_bias[0], zeros4, mlstm_i_bias[0], mlstm_f_bias[0], pad]
                            ).astype(F32).reshape(1, GATE_COLS)
    alog = jnp.concatenate([gdn_a_log[0], zeros4, zeros4, zeros4, pad]).astype(F32).reshape(1, GATE_COLS)

    kmem, vmem = _memkv(mem2d, norm_mem_w[0].reshape(1, d), xa_wkv[0].astype(BF16))
    qkv, zvo, gates = _inproj(xs, norm_mix_w[0].reshape(1, d), w_all, cw, gbias, alog,
                              seq=seq, tm=tm)
    go = _gdn(qkv, gates, batch=batch, seq=seq, ts=ts)
    mh = _mlstm(qkv, zvo, gates, batch=batch, seq=seq, ts=ts)
    x2 = _mix_xattn(xs, go, mh, zvo, gdn_norm_w[0].reshape(1, HEAD_DIM),
                    mlstm_norm_w[0].reshape(1, MIX_W), w_out[0].astype(BF16),
                    norm_xa_w[0].reshape(1, d), xa_wq[0].astype(BF16),
                    kmem.reshape(batch, MEM_TOKENS, d), vmem.reshape(batch, MEM_TOKENS, d),
                    xa_wo[0].astype(BF16), seq=seq, tm=min(2 * tm, seq))
    xn, c0, ap, r1, bp = _peer_route(x2, norm_ffn_w[0].reshape(1, d), peer_wq[0].astype(BF16),
                                     peer_sub_keys[0].astype(BF16), tt=min(2 * tt, t))
    u16 = (peer_u[0] * math.sqrt(0.5)).astype(BF16)
    out = _peer_eval(xn, x2, u16, peer_v[0].astype(BF16), c0, ap, r1, bp,
                     norm_final_w.reshape(1, d), tt=tt, eb=4096, sb=256)
    return out.reshape(batch, seq, d)
```
